```python
import math
import jax, jax.numpy as jnp
from jax import lax
import numpy as np

D_MODEL = 1024
BATCH = 16
SEQ = 2048
DEPTH = 4

N_MIXERS = 3
N_SSD = (DEPTH + 2) // 3
N_SC = (DEPTH + 1) // 3
N_NA = DEPTH // 3

SSD_EXPAND = 2
SSD_D_INNER = SSD_EXPAND * D_MODEL
SSD_HEAD_DIM = 64
SSD_HEADS = SSD_D_INNER // SSD_HEAD_DIM
SSD_GROUPS = 8
SSD_STATE = 128
SSD_CONV = 5
SSD_CHUNK = 128
SSD_CONV_DIM = SSD_D_INNER + 2 * SSD_GROUPS * SSD_STATE
SSD_IN_DIM = SSD_D_INNER + SSD_CONV_DIM + 2 * SSD_HEADS
DT_MIN = 1e-3
DT_MAX = 1e-1

SC_CONV = 3

GRID_W = 64
NA_HEADS = 16
NA_HEAD_DIM = D_MODEL // NA_HEADS
NA_WIN_ROWS = 8
NA_WIN_COLS = 16
NA_QCOL_BLOCK = 16
NA_KEY_COLS = NA_QCOL_BLOCK + NA_WIN_COLS

FFN_DIM = 4 * D_MODEL

EPS = 1e-5

kernel_name = 'hybrid_ssd_shortconv_natten_encoder'


def layer_norm(x, g, b):
    xf = x.astype(jnp.float32)
    mu = jnp.mean(xf, axis=-1, keepdims=True)
    var = jnp.mean(jnp.square(xf - mu), axis=-1, keepdims=True)
    y = (xf - mu) * lax.rsqrt(var + EPS)
    return y.astype(x.dtype) * g + b


def grouped_rmsnorm(y, w):
    shp = y.shape
    yg = y.reshape(shp[:-1] + (SSD_GROUPS, shp[-1] // SSD_GROUPS)).astype(jnp.float32)
    yg = yg * lax.rsqrt(jnp.mean(jnp.square(yg), axis=-1, keepdims=True) + EPS)
    return yg.reshape(shp).astype(y.dtype) * w


def depthwise_conv(x, w):
    k = w.shape[0]
    return lax.conv_general_dilated(
        x, w[:, None, :].astype(x.dtype), window_strides=(1,), padding=[(k // 2, k // 2)],
        dimension_numbers=('NWC', 'WIO', 'NWC'), feature_group_count=x.shape[-1])


def ssd_chunked(xs, dt, a_coef, bm, cm):
    seq = xs.shape[0]
    nc = seq // SSD_CHUNK
    hpg = SSD_HEADS // SSD_GROUPS
    xdt = (xs * dt[..., None]).reshape(nc, SSD_CHUNK, SSD_GROUPS, hpg, SSD_HEAD_DIM)
    a = (dt * a_coef).astype(jnp.float32).reshape(nc, SSD_CHUNK, SSD_GROUPS, hpg)
    a_cs = jnp.cumsum(a, axis=1)
    bc = bm.reshape(nc, SSD_CHUNK, SSD_GROUPS, SSD_STATE)
    cc = cm.reshape(nc, SSD_CHUNK, SSD_GROUPS, SSD_STATE)
    lower = jnp.tril(jnp.ones((SSD_CHUNK, SSD_CHUNK), dtype=bool))
    seg = a_cs[:, :, None] - a_cs[:, None, :]
    decay = jnp.exp(jnp.where(lower[None, :, :, None, None], seg, -jnp.inf))
    cb = jnp.einsum('clgn,csgn->clsg', cc, bc)
    y_diag = jnp.einsum('clsg,clsgk,csgkp->clgkp', cb, decay, xdt)
    decay_to_end = jnp.exp(a_cs[:, -1:] - a_cs)
    states = jnp.einsum('csgn,csgk,csgkp->cgkpn', bc, decay_to_end, xdt)
    chunk_decay = jnp.exp(a_cs[:, -1])

    def step(carry, inp):
        st, dec = inp
        return carry * dec[..., None, None] + st, carry

    init = jnp.zeros(states.shape[1:], states.dtype)
    _, prev = lax.scan(step, init, (states, chunk_decay))
    y_off = jnp.einsum('clgn,cgkpn,clgk->clgkp', cc, prev, jnp.exp(a_cs))
    return (y_diag + y_off).reshape(seq, SSD_HEADS, SSD_HEAD_DIM).astype(xs.dtype)


def run_ssd(xs, dt, a_coef, bm, cm):
    return lax.map(lambda t: ssd_chunked(t[0], t[1], a_coef, t[2], t[3]), (xs, dt, bm, cm))


def ssd_mixer(x, w_in, conv_w, conv_b, dt_bias, a_log, d_skip, norm_w, w_out):
    bsz, seq, _ = x.shape
    zxbcdt = x @ w_in
    z = zxbcdt[..., :SSD_D_INNER]
    xbc = zxbcdt[..., SSD_D_INNER:SSD_D_INNER + SSD_CONV_DIM]
    dt_raw = zxbcdt[..., SSD_D_INNER + SSD_CONV_DIM:]
    xbc = jax.nn.silu(depthwise_conv(xbc, conv_w) + conv_b)
    gn = SSD_GROUPS * SSD_STATE
    xs = xbc[..., :SSD_D_INNER].reshape(bsz, seq, SSD_HEADS, SSD_HEAD_DIM)
    bm = xbc[..., SSD_D_INNER:SSD_D_INNER + gn].reshape(bsz, seq, SSD_GROUPS, SSD_STATE)
    cm = xbc[..., SSD_D_INNER + gn:].reshape(bsz, seq, SSD_GROUPS, SSD_STATE)
    dt = jax.nn.softplus(dt_raw.reshape(bsz, seq, 2, SSD_HEADS) + dt_bias)
    a_coef = -jnp.exp(a_log)
    y_fwd = run_ssd(xs, dt[:, :, 0], a_coef[0], bm, cm)
    flip = lambda t: jnp.flip(t, axis=1)
    y_bwd = flip(run_ssd(flip(xs), flip(dt[:, :, 1]), a_coef[1], flip(bm), flip(cm)))
    y = y_fwd + y_bwd + d_skip[:, None] * xs
    y = y.reshape(bsz, seq, SSD_D_INNER) * jax.nn.silu(z)
    return grouped_rmsnorm(y, norm_w) @ w_out


def short_conv_mixer(x, w_in, conv_w, w_out):
    d = x.shape[-1]
    bch = x @ w_in
    b_gate, c_gate, h = bch[..., :d], bch[..., d:2 * d], bch[..., 2 * d:]
    return (b_gate * depthwise_conv(c_gate * h, conv_w)) @ w_out


def na_mixer(x, w_qkv, rpb, w_out):
    bsz, seq, d = x.shape
    rows = seq // GRID_W
    wr = min(NA_WIN_ROWS, rows)
    nqb = GRID_W // NA_QCOL_BLOCK
    qkv = (x @ w_qkv).reshape(bsz, rows, GRID_W, 3, NA_HEADS, NA_HEAD_DIM)
    q = qkv[:, :, :, 0] * (NA_HEAD_DIM ** -0.5)
    k = qkv[:, :, :, 1]
    v = qkv[:, :, :, 2]
    r = jnp.arange(rows)
    key_rows = jnp.clip(r - wr // 2, 0, rows - wr)[:, None] + jnp.arange(wr)
    qcol = (jnp.arange(nqb) * NA_QCOL_BLOCK)[:, None] + jnp.arange(NA_QCOL_BLOCK)
    key_cols = (jnp.clip(qcol[:, 0] - NA_WIN_COLS // 2, 0, GRID_W - NA_KEY_COLS)[:, None]
                + jnp.arange(NA_KEY_COLS))
    col_start = jnp.clip(qcol - NA_WIN_COLS // 2, 0, GRID_W - NA_WIN_COLS)
    in_win = ((key_cols[:, None, :] >= col_start[..., None])
              & (key_cols[:, None, :] < col_start[..., None] + NA_WIN_COLS))
    dr_idx = key_rows - r[:, None] + NA_WIN_ROWS - 1
    dc_idx = jnp.clip(key_cols[:, None, :] - qcol[:, :, None] + NA_WIN_COLS - 1,
                      0, 2 * NA_WIN_COLS - 2)
    bias = rpb[:, dr_idx[:, None, None, :, None], dc_idx[None, :, :, None, :]]
    bias = jnp.where(in_win[None, None, :, :, None, :], bias.astype(jnp.float32), -jnp.inf)

    def one(t):
        qs, ks, vs = t
        qb = qs.reshape(rows, nqb, NA_QCOL_BLOCK, NA_HEADS, NA_HEAD_DIM)
        kg = ks[key_rows[:, None, :, None], key_cols[None, :, None, :]]
        vg = vs[key_rows[:, None, :, None], key_cols[None, :, None, :]]
        s = jnp.einsum('rcqhd,rcikhd->hrcqik', qb, kg).astype(jnp.float32) + bias
        shp = s.shape
        p = jax.nn.softmax(s.reshape(shp[:4] + (shp[4] * shp[5],)), axis=-1).reshape(shp)
        o = jnp.einsum('hrcqik,rcikhd->rcqhd', p.astype(vs.dtype), vg)
        return o.reshape(rows * GRID_W, NA_HEADS * NA_HEAD_DIM)

    return lax.map(one, (q, k, v)) @ w_out


def sq_relu_mlp(x, w_in, w_out):
    return jnp.square(jax.nn.relu(x @ w_in)) @ w_out


def setup_inputs(seed: int = 0) -> dict:
    key = jax.random.key(seed)
    ks = jax.random.split(key, 24)
    f32 = jnp.float32
    beta = (8.0 * DEPTH) ** -0.25
    nrm = lambda k, shape, scale: jax.random.normal(k, shape, f32) * scale
    x = nrm(ks[0], (BATCH, SEQ, D_MODEL), 1.0)
    ln_mix_g = 1.0 + nrm(ks[1], (DEPTH, D_MODEL), 0.02)
    ln_mix_b = nrm(ks[2], (DEPTH, D_MODEL), 0.02)
    ln_ffn_g = 1.0 + nrm(ks[3], (DEPTH, D_MODEL), 0.02)
    ln_ffn_b = nrm(ks[4], (DEPTH, D_MODEL), 0.02)
    ffn_w_in = nrm(ks[5], (DEPTH, D_MODEL, FFN_DIM), D_MODEL ** -0.5)
    ffn_w_out = nrm(ks[6], (DEPTH, FFN_DIM, D_MODEL), beta * FFN_DIM ** -0.5)
    ssd_w_in = nrm(ks[7], (N_SSD, D_MODEL, SSD_IN_DIM), D_MODEL ** -0.5)
    ssd_conv_w = nrm(ks[8], (N_SSD, SSD_CONV, SSD_CONV_DIM), SSD_CONV ** -0.5)
    ssd_conv_b = nrm(ks[9], (N_SSD, SSD_CONV_DIM), 0.01)
    dt0 = jnp.exp(jax.random.uniform(ks[10], (N_SSD, 2, SSD_HEADS), f32,
                                     math.log(DT_MIN), math.log(DT_MAX)))
    ssd_dt_bias = dt0 + jnp.log(-jnp.expm1(-dt0))
    ssd_a_log = jnp.log(jax.random.uniform(ks[11], (N_SSD, 2, SSD_HEADS), f32, 1.0, 16.0))
    ssd_d = 1.0 + nrm(ks[12], (N_SSD, SSD_HEADS), 0.02)
    ssd_norm_w = 1.0 + nrm(ks[13], (N_SSD, SSD_D_INNER), 0.02)
    ssd_w_out = nrm(ks[14], (N_SSD, SSD_D_INNER, D_MODEL), beta * SSD_D_INNER ** -0.5)
    sc_w_in = nrm(ks[15], (N_SC, D_MODEL, 3 * D_MODEL), D_MODEL ** -0.5)
    sc_conv_w = nrm(ks[16], (N_SC, SC_CONV, D_MODEL), SC_CONV ** -0.5)
    sc_w_out = nrm(ks[17], (N_SC, D_MODEL, D_MODEL), beta * D_MODEL ** -0.5)
    na_w_qkv = nrm(ks[18], (N_NA, D_MODEL, 3 * D_MODEL), D_MODEL ** -0.5)
    na_rpb = nrm(ks[19], (N_NA, NA_HEADS, 2 * NA_WIN_ROWS - 1, 2 * NA_WIN_COLS - 1), 0.02)
    na_w_out = nrm(ks[20], (N_NA, D_MODEL, D_MODEL), beta * D_MODEL ** -0.5)
    return {'x': x, 'ln_mix_g': ln_mix_g, 'ln_mix_b': ln_mix_b, 'ln_ffn_g': ln_ffn_g,
            'ln_ffn_b': ln_ffn_b, 'ffn_w_in': ffn_w_in, 'ffn_w_out': ffn_w_out,
            'ssd_w_in': ssd_w_in, 'ssd_conv_w': ssd_conv_w, 'ssd_conv_b': ssd_conv_b,
            'ssd_dt_bias': ssd_dt_bias, 'ssd_a_log': ssd_a_log, 'ssd_d': ssd_d,
            'ssd_norm_w': ssd_norm_w, 'ssd_w_out': ssd_w_out,
            'sc_w_in': sc_w_in, 'sc_conv_w': sc_conv_w, 'sc_w_out': sc_w_out,
            'na_w_qkv': na_w_qkv, 'na_rpb': na_rpb, 'na_w_out': na_w_out}


def reference(x, ln_mix_g, ln_mix_b, ln_ffn_g, ln_ffn_b, ffn_w_in, ffn_w_out,
              ssd_w_in, ssd_conv_w, ssd_conv_b, ssd_dt_bias, ssd_a_log, ssd_d,
              ssd_norm_w, ssd_w_out, sc_w_in, sc_conv_w, sc_w_out,
              na_w_qkv, na_rpb, na_w_out):
    alpha = (2.0 * DEPTH) ** 0.25
    for i in range(DEPTH):
        kind = i % N_MIXERS
        j = i // N_MIXERS
        if kind == 0:
            h = ssd_mixer(x, ssd_w_in[j], ssd_conv_w[j], ssd_conv_b[j], ssd_dt_bias[j],
                          ssd_a_log[j], ssd_d[j], ssd_norm_w[j], ssd_w_out[j])
        elif kind == 1:
            h = short_conv_mixer(x, sc_w_in[j], sc_conv_w[j], sc_w_out[j])
        else:
            h = na_mixer(x, na_w_qkv[j], na_rpb[j], na_w_out[j])
        x = layer_norm(alpha * x + h, ln_mix_g[i], ln_mix_b[i])
        x = layer_norm(alpha * x + sq_relu_mlp(x, ffn_w_in[i], ffn_w_out[i]),
                       ln_ffn_g[i], ln_ffn_b[i])
    return x
```

```python
import functools

import jax
import jax.numpy as jnp
from jax import lax
from jax.experimental import pallas as pl
from jax.experimental.pallas import tpu as pltpu

F32 = jnp.float32
BF16 = jnp.bfloat16

D_MODEL = 1024
DEPTH = 4
N_MIXERS = 3
FFN_DIM = 4 * D_MODEL
EPS = 1e-5

SSD_D_INNER = 2 * D_MODEL
SSD_HEAD_DIM = 64
SSD_HEADS = SSD_D_INNER // SSD_HEAD_DIM
SSD_GROUPS = 8
SSD_HPG = SSD_HEADS // SSD_GROUPS
SSD_GW = SSD_HPG * SSD_HEAD_DIM
SSD_STATE = 128
SSD_CONV = 5
SSD_CHUNK = 128
SSD_GN = SSD_GROUPS * SSD_STATE
SSD_CONV_DIM = SSD_D_INNER + 2 * SSD_GN
SSD_DT_PAD = 128

SC_CONV = 3

GRID_W = 64
NA_HEADS = 16
NA_HEAD_DIM = D_MODEL // NA_HEADS
NA_WIN_ROWS = 8
NA_WIN_COLS = 16
NA_KEYS = NA_WIN_ROWS * GRID_W

HALO = 8
VMEM_LIMIT = 56 * 1024 * 1024


def _cparams(n_axes):
    return pltpu.CompilerParams(dimension_semantics=("arbitrary",) * n_axes,
                                vmem_limit_bytes=VMEM_LIMIT)


def _resident(shape):
    nd = len(shape)
    return pl.BlockSpec(shape, lambda *_: (0,) * nd, pipeline_mode=pl.Buffered(1))


def _dot(a, b):
    return jnp.dot(a, b, preferred_element_type=F32)


def _sigmoid(x):
    return 1.0 / (1.0 + jnp.exp(-x))


def _layer_norm(y, g, b):
    mu = jnp.mean(y, axis=-1, keepdims=True)
    yc = y - mu
    var = jnp.mean(yc * yc, axis=-1, keepdims=True)
    return yc * lax.rsqrt(var + EPS) * g + b


def _halo_specs(tm, n_rows, width):
    blocks_per_tile = tm // HALO
    last_block = n_rows // HALO - 1
    prev = pl.BlockSpec((HALO, width), lambda i: (jnp.maximum(i * blocks_per_tile - 1, 0), 0))
    cur = pl.BlockSpec((tm, width), lambda i: (i, 0))
    nxt = pl.BlockSpec((HALO, width), lambda i: (jnp.minimum((i + 1) * blocks_per_tile, last_block), 0))
    return prev, cur, nxt


def _halo_valid(tm, tiles_per_seq):
    i = pl.program_id(0)
    first = (i % tiles_per_seq) == 0
    last = (i % tiles_per_seq) == tiles_per_seq - 1
    rows = lax.broadcasted_iota(jnp.int32, (tm + 2 * HALO, 1), 0)
    ok_top = jnp.logical_or(rows >= HALO, jnp.logical_not(first))
    ok_bot = jnp.logical_or(rows < tm + HALO, jnp.logical_not(last))
    return jnp.logical_and(ok_top, ok_bot)


def _mlp_ln_kernel(x_ref, w1_ref, w2_ref, g_ref, b_ref, o_ref, *, alpha, ff_chunk):
    x = x_ref[...]
    xb = x.astype(BF16)
    acc = alpha * x
    for c in range(FFN_DIM // ff_chunk):
        sl = slice(c * ff_chunk, (c + 1) * ff_chunk)
        h = jnp.maximum(_dot(xb, w1_ref[:, sl]), 0.0)
        acc = acc + _dot((h * h).astype(BF16), w2_ref[sl, :])
    o_ref[...] = _layer_norm(acc, g_ref[...], b_ref[...])


def _mlp_ln(x2d, w1, w2, g, b, alpha, tm=512, ff_chunk=1024):
    n, d = x2d.shape
    return pl.pallas_call(
        functools.partial(_mlp_ln_kernel, alpha=alpha, ff_chunk=ff_chunk),
        grid=(n // tm,),
        in_specs=[pl.BlockSpec((tm, d), lambda i: (i, 0)),
                  _resident(w1.shape), _resident(w2.shape),
                  _resident((1, d)), _resident((1, d))],
        out_specs=pl.BlockSpec((tm, d), lambda i: (i, 0)),
        out_shape=jax.ShapeDtypeStruct((n, d), F32),
        compiler_params=_cparams(1),
        name="mlp_ln",
    )(x2d, w1, w2, g.reshape(1, d), b.reshape(1, d))


def _proj_ln_kernel(y_ref, x_ref, w_ref, g_ref, b_ref, o_ref, *, alpha):
    acc = alpha * x_ref[...] + _dot(y_ref[...], w_ref[...])
    o_ref[...] = _layer_norm(acc, g_ref[...], b_ref[...])


def _proj_ln(y2d, x2d, w, g, b, alpha, tm=512):
    n, d = x2d.shape
    k = y2d.shape[1]
    return pl.pallas_call(
        functools.partial(_proj_ln_kernel, alpha=alpha),
        grid=(n // tm,),
        in_specs=[pl.BlockSpec((tm, k), lambda i: (i, 0)),
                  pl.BlockSpec((tm, d), lambda i: (i, 0)),
                  _resident(w.shape), _resident((1, d)), _resident((1, d))],
        out_specs=pl.BlockSpec((tm, d), lambda i: (i, 0)),
        out_shape=jax.ShapeDtypeStruct((n, d), F32),
        compiler_params=_cparams(1),
        name="proj_ln",
    )(y2d, x2d, w, g.reshape(1, d), b.reshape(1, d))


def _sc_kernel(xp_ref, x_ref, xn_ref, win_ref, cw_ref, wout_ref, g_ref, b_ref, o_ref, v_scr,
               *, tm, tiles_per_seq, alpha, col_chunk):
    d = D_MODEL
    n = tm + 2 * HALO
    x = x_ref[...]
    xcat = jnp.concatenate([xp_ref[...], x, xn_ref[...]], axis=0).astype(BF16)
    xb = xcat[HALO:HALO + tm]
    valid = _halo_valid(tm, tiles_per_seq)
    for j in range(d // col_chunk):
        lo = j * col_chunk
        b_gate = _dot(xb, win_ref[:, lo:lo + col_chunk])
        c_gate = _dot(xcat, win_ref[:, d + lo:d + lo + col_chunk])
        h = _dot(xcat, win_ref[:, 2 * d + lo:2 * d + lo + col_chunk])
        u = jnp.where(valid, c_gate * h, 0.0)
        cw = cw_ref[:, lo:lo + col_chunk]
        conv = (cw[0:1] * pltpu.roll(u, 1, 0) + cw[1:2] * u + cw[2:3] * pltpu.roll(u, n - 1, 0))
        v_scr[:, lo:lo + col_chunk] = (b_gate * conv[HALO:HALO + tm]).astype(BF16)
    acc = alpha * x + _dot(v_scr[...], wout_ref[...])
    o_ref[...] = _layer_norm(acc, g_ref[...], b_ref[...])


def _sc_layer(x2d, seq, w_in, conv_w, w_out, g, b, alpha, tm=512, col_chunk=256):
    n, d = x2d.shape
    prev, cur, nxt = _halo_specs(tm, n, d)
    return pl.pallas_call(
        functools.partial(_sc_kernel, tm=tm, tiles_per_seq=seq // tm, alpha=alpha, col_chunk=col_chunk),
        grid=(n // tm,),
        in_specs=[prev, cur, nxt, _resident(w_in.shape), _resident(conv_w.shape), _resident(w_out.shape),
                  _resident((1, d)), _resident((1, d))],
        out_specs=pl.BlockSpec((tm, d), lambda i: (i, 0)),
        out_shape=jax.ShapeDtypeStruct((n, d), F32),
        scratch_shapes=[pltpu.VMEM((tm, d), BF16)],
        compiler_params=_cparams(1),
        name="short_conv_layer",
    )(x2d, x2d, x2d, w_in, conv_w, w_out, g.reshape(1, d), b.reshape(1, d))


def _ssd_in_kernel(xp_ref, x_ref, xn_ref, wz_ref, wxbc_ref, wdt_ref, cw_ref, cb_ref,
                   z_ref, xs_ref, bm_ref, cm_ref, dt_ref, *, tm, tiles_per_seq, col_chunk):
    n = tm + 2 * HALO
    xcat = jnp.concatenate([xp_ref[...], x_ref[...], xn_ref[...]], axis=0).astype(BF16)
    xb = xcat[HALO:HALO + tm]
    valid = _halo_valid(tm, tiles_per_seq)
    for j in range(SSD_D_INNER // col_chunk):
        lo = j * col_chunk
        z_ref[:, lo:lo + col_chunk] = _dot(xb, wz_ref[:, lo:lo + col_chunk]).astype(BF16)
    dt_ref[...] = _dot(xb, wdt_ref[...])
    for j in range(SSD_CONV_DIM // col_chunk):
        lo = j * col_chunk
        y = jnp.where(valid, _dot(xcat, wxbc_ref[:, lo:lo + col_chunk]), 0.0)
        cw = cw_ref[:, lo:lo + col_chunk]
        conv = (cw[0:1] * pltpu.roll(y, 2, 0) + cw[1:2] * pltpu.roll(y, 1, 0) + cw[2:3] * y
                + cw[3:4] * pltpu.roll(y, n - 1, 0) + cw[4:5] * pltpu.roll(y, n - 2, 0))
        conv = conv[HALO:HALO + tm] + cb_ref[:, lo:lo + col_chunk]
        act = (conv * _sigmoid(conv)).astype(BF16)
        if lo < SSD_D_INNER:
            xs_ref[:, lo:lo + col_chunk] = act
        elif lo < SSD_D_INNER + SSD_GN:
            bm_ref[:, lo - SSD_D_INNER:lo - SSD_D_INNER + col_chunk] = act
        else:
            o = lo - SSD_D_INNER - SSD_GN
            cm_ref[:, o:o + col_chunk] = act


def _ssd_in(x2d, seq, wz, wxbc, wdt, conv_w, conv_b, tm=512, col_chunk=512):
    n, d = x2d.shape
    prev, cur, nxt = _halo_specs(tm, n, d)
    row = lambda w: pl.BlockSpec((tm, w), lambda i: (i, 0))
    return pl.pallas_call(
        functools.partial(_ssd_in_kernel, tm=tm, tiles_per_seq=seq // tm, col_chunk=col_chunk),
        grid=(n // tm,),
        in_specs=[prev, cur, nxt, _resident(wz.shape), _resident(wxbc.shape), _resident(wdt.shape),
                  _resident(conv_w.shape), _resident((1, SSD_CONV_DIM))],
        out_specs=[row(SSD_D_INNER), row(SSD_D_INNER), row(SSD_GN), row(SSD_GN), row(SSD_DT_PAD)],
        out_shape=[jax.ShapeDtypeStruct((n, SSD_D_INNER), BF16), jax.ShapeDtypeStruct((n, SSD_D_INNER), BF16),
                   jax.ShapeDtypeStruct((n, SSD_GN), BF16), jax.ShapeDtypeStruct((n, SSD_GN), BF16),
                   jax.ShapeDtypeStruct((n, SSD_DT_PAD), F32)],
        compiler_params=_cparams(1),
        name="ssd_in_proj_conv",
    )(x2d, x2d, x2d, wz, wxbc, wdt, conv_w, conv_b.reshape(1, SSD_CONV_DIM))


def _ssd_prep_kernel(dt_ref, bias_ref, acoef_ref, p1_ref, p2_ref, p3_ref, r1_ref, r2_ref, r4_ref, *, tl):
    rows = lax.broadcasted_iota(jnp.int32, (SSD_CHUNK, 1), 0)
    lane = lax.broadcasted_iota(jnp.int32, (1, SSD_DT_PAD), 1)
    is_fwd = (lane // SSD_HPG) % 2 == 0
    for c in range(tl // SSD_CHUNK):
        sl = slice(c * SSD_CHUNK, (c + 1) * SSD_CHUNK)
        raw = dt_ref[0, sl, :] + bias_ref[...]
        dt = jnp.maximum(raw, 0.0) + jnp.log1p(jnp.exp(-jnp.abs(raw)))
        a = dt * acoef_ref[...]
        cs = a
        sh = 1
        while sh < SSD_CHUNK:
            cs = cs + jnp.where(rows >= sh, pltpu.roll(cs, sh, 0), 0.0)
            sh *= 2
        tot = cs[SSD_CHUNK - 1:SSD_CHUNK, :]
        ec = cs - a
        p1 = jnp.where(is_fwd, cs, ec)
        p2 = jnp.where(is_fwd, jnp.exp(cs), jnp.exp(tot - ec))
        p3 = jnp.where(is_fwd, jnp.exp(tot - cs), jnp.exp(ec)) * dt
        etot = jnp.broadcast_to(jnp.exp(tot), (SSD_CHUNK, SSD_DT_PAD))
        p1_ref[0, sl, :] = p1
        p2_ref[0, sl, :] = p2
        p3_ref[0, sl, :] = p3
        r1_ref[0, :, sl] = p1.T
        r2_ref[0, :, sl] = dt.T
        r4_ref[0, :, sl] = etot.T


def _ssd_prep(dt_raw, bias, acoef, tl=1024):
    bsz, seq, w = dt_raw.shape
    col = pl.BlockSpec((1, tl, w), lambda b, i: (b, i, 0))
    rowm = pl.BlockSpec((1, w, tl), lambda b, i: (b, 0, i))
    col_shape = jax.ShapeDtypeStruct((bsz, seq, w), F32)
    row_shape = jax.ShapeDtypeStruct((bsz, w, seq), F32)
    return pl.pallas_call(
        functools.partial(_ssd_prep_kernel, tl=tl),
        grid=(bsz, seq // tl),
        in_specs=[col, pl.BlockSpec((1, w), lambda b, i: (0, 0)), pl.BlockSpec((1, w), lambda b, i: (0, 0))],
        out_specs=[col, col, col, rowm, rowm, rowm],
        out_shape=[col_shape, col_shape, col_shape, row_shape, row_shape, row_shape],
        compiler_params=_cparams(2),
        name="ssd_decay_tables",
    )(dt_raw, bias, acoef)


def _head_lanes(vals, lane):
    out = vals[SSD_HPG - 1]
    for k in range(SSD_HPG - 2, -1, -1):
        out = jnp.where(lane < (k + 1) * SSD_HEAD_DIM, vals[k], out)
    return out


def _ssd_core_kernel(xs_ref, bm_ref, cm_ref, z_ref, p1_ref, p2_ref, p3_ref, r1_ref, r2_ref, r4_ref,
                     dsk_ref, nw_ref, o_ref, st_scr, *, nchunks):
    q = SSD_CHUNK
    hpg = SSD_HPG
    lane = lax.broadcasted_iota(jnp.int32, (1, SSD_GW), 1)

    def col_factor(p_ref, rs, direction):
        blk = p_ref[0, 0, rs, :]
        return _head_lanes([blk[:, direction * hpg + k:direction * hpg + k + 1] for k in range(hpg)], lane)

    def chunk_decay(c, direction):
        blk = r4_ref[0, 0, :, c * q:(c + 1) * q]
        rows = [jnp.concatenate([blk[direction * hpg + k:direction * hpg + k + 1, :]] * (SSD_GW // q), axis=1)
                for k in range(hpg)]
        return _head_lanes(rows, lane)

    def state_body(c, carry):
        rs = pl.ds(pl.multiple_of(c * q, q), q)
        xs = xs_ref[0, rs, :].astype(F32)
        xw = jnp.concatenate([(xs * col_factor(p3_ref, rs, 0)).astype(BF16),
                              (xs * col_factor(p3_ref, rs, 1)).astype(BF16)], axis=1)
        bt = bm_ref[0, rs, :].astype(F32).T.astype(BF16)
        st = _dot(bt, xw)
        st_scr[c, 0] = st[:, :SSD_GW]
        st_scr[c, 1] = st[:, SSD_GW:]
        return carry

    lax.fori_loop(0, nchunks, state_body, 0)

    s_f = jnp.zeros((SSD_STATE, SSD_GW), F32)
    for c in range(nchunks):
        add = st_scr[c, 0]
        st_scr[c, 0] = s_f
        s_f = chunk_decay(c, 0) * s_f + add
    s_b = jnp.zeros((SSD_STATE, SSD_GW), F32)
    for c in range(nchunks - 1, -1, -1):
        add = st_scr[c, 1]
        st_scr[c, 1] = s_b
        s_b = chunk_decay(c, 1) * s_b + add

    ti = lax.broadcasted_iota(jnp.int32, (q, q), 0)
    si = lax.broadcasted_iota(jnp.int32, (q, q), 1)
    lower = si <= ti
    upper = si >= ti
    neg_inf = jnp.float32(-jnp.inf)

    def out_body(c, carry):
        rs = pl.ds(pl.multiple_of(c * q, q), q)
        xs_b = xs_ref[0, rs, :]
        cm = cm_ref[0, rs, :]
        g = lax.dot_general(cm, bm_ref[0, rs, :], (((1,), (1,)), ((), ())), preferred_element_type=F32)
        p1 = p1_ref[0, 0, rs, :]
        r1 = r1_ref[0, 0, :, rs]
        r2 = r2_ref[0, 0, :, rs]
        ms = []
        for k in range(hpg):
            seg_f = p1[:, k:k + 1] - r1[k:k + 1, :]
            seg_b = r1[hpg + k:hpg + k + 1, :] - p1[:, hpg + k:hpg + k + 1]
            l_f = jnp.exp(jnp.where(lower, seg_f, neg_inf)) * r2[k:k + 1, :]
            l_b = jnp.exp(jnp.where(upper, seg_b, neg_inf)) * r2[hpg + k:hpg + k + 1, :]
            ms.append((g * (l_f + l_b)).astype(BF16))
        zero = jnp.zeros_like(xs_b)
        xbd = jnp.concatenate(
            [jnp.where(jnp.logical_and(lane >= k * SSD_HEAD_DIM, lane < (k + 1) * SSD_HEAD_DIM), xs_b, zero)
             for k in range(hpg)], axis=0)
        y = _dot(jnp.concatenate(ms, axis=1), xbd)
        s_in = jnp.concatenate([st_scr[c, 0], st_scr[c, 1]], axis=1).astype(BF16)
        y_off = _dot(cm, s_in)
        xs = xs_b.astype(F32)
        y = (y + col_factor(p2_ref, rs, 0) * y_off[:, :SSD_GW] + col_factor(p2_ref, rs, 1) * y_off[:, SSD_GW:]
             + dsk_ref[0] * xs)
        z = z_ref[0, rs, :].astype(F32)
        y = y * (z * _sigmoid(z))
        y = y * lax.rsqrt(jnp.mean(y * y, axis=-1, keepdims=True) + EPS)
        o_ref[0, rs, :] = (y * nw_ref[0]).astype(BF16)
        return carry

    lax.fori_loop(0, nchunks, out_body, 0)


def _ssd_core(xs, bm, cm, z, p1, p2, p3, r1, r2, r4, dsk, nw):
    bsz, seq, _ = xs.shape
    nchunks = seq // SSD_CHUNK
    wide = pl.BlockSpec((1, seq, SSD_GW), lambda b, g: (b, 0, g))
    narrow = pl.BlockSpec((1, seq, SSD_STATE), lambda b, g: (b, 0, g))
    colt = pl.BlockSpec((1, 1, seq, 2 * SSD_HPG), lambda b, g: (b, g, 0, 0))
    rowt = pl.BlockSpec((1, 1, 2 * SSD_HPG, seq), lambda b, g: (b, g, 0, 0))
    pergroup = pl.BlockSpec((1, 1, SSD_GW), lambda b, g: (g, 0, 0))
    return pl.pallas_call(
        functools.partial(_ssd_core_kernel, nchunks=nchunks),
        grid=(bsz, SSD_GROUPS),
        in_specs=[wide, narrow, narrow, wide, colt, colt, colt, rowt, rowt, rowt, pergroup, pergroup],
        out_specs=wide,
        out_shape=jax.ShapeDtypeStruct((bsz, seq, SSD_D_INNER), BF16),
        scratch_shapes=[pltpu.VMEM((nchunks, 2, SSD_STATE, SSD_GW), F32)],
        compiler_params=_cparams(2),
        name="ssd_scan",
    )(xs, bm, cm, z, p1, p2, p3, r1, r2, r4, dsk, nw)


def _ssd_layer(x2d, bsz, seq, w_in, conv_w, conv_b, dt_bias, a_log, d_skip, norm_w, w_out, g, b, alpha):
    n = x2d.shape[0]
    o_xbc = SSD_D_INNER
    o_dt = SSD_D_INNER + SSD_CONV_DIM
    perm = jnp.array([dr * SSD_HEADS + gi * SSD_HPG + k
                      for gi in range(SSD_GROUPS) for dr in range(2) for k in range(SSD_HPG)], jnp.int32)
    pad = SSD_DT_PAD - 2 * SSD_HEADS
    wz = w_in[:, :o_xbc].astype(BF16)
    wxbc = w_in[:, o_xbc:o_dt].astype(BF16)
    wdt = jnp.pad(w_in[:, o_dt:][:, perm], ((0, 0), (0, pad))).astype(BF16)
    bias = jnp.pad(dt_bias.reshape(-1)[perm], (0, pad)).reshape(1, SSD_DT_PAD)
    acoef = jnp.pad(a_log.reshape(-1)[perm], (0, pad)).reshape(1, SSD_DT_PAD)
    acoef = jnp.where(jnp.arange(SSD_DT_PAD) < 2 * SSD_HEADS, -jnp.exp(acoef), 0.0).reshape(1, SSD_DT_PAD)

    z, xs, bm, cm, dt_raw = _ssd_in(x2d, seq, wz, wxbc, wdt, conv_w, conv_b)
    tables = _ssd_prep(dt_raw.reshape(bsz, seq, SSD_DT_PAD), bias, acoef)
    nhd = 2 * SSD_HPG
    cols = [t[:, :, :2 * SSD_HEADS].reshape(bsz, seq, SSD_GROUPS, nhd).transpose(0, 2, 1, 3) for t in tables[:3]]
    rows = [t[:, :2 * SSD_HEADS, :].reshape(bsz, SSD_GROUPS, nhd, seq) for t in tables[3:]]
    dsk = jnp.repeat(d_skip, SSD_HEAD_DIM).reshape(SSD_GROUPS, 1, SSD_GW)
    nw = norm_w.reshape(SSD_GROUPS, 1, SSD_GW)
    y = _ssd_core(xs.reshape(bsz, seq, -1), bm.reshape(bsz, seq, -1), cm.reshape(bsz, seq, -1),
                  z.reshape(bsz, seq, -1), *cols, *rows, dsk, nw)
    return _proj_ln(y.reshape(n, SSD_D_INNER), x2d, w_out.astype(BF16), g, b, alpha)


def _qkv_kernel(x_ref, w_ref, q_ref, k_ref, v_ref, *, col_chunk):
    d = D_MODEL
    xb = x_ref[...].astype(BF16)
    for j in range(d // col_chunk):
        lo = j * col_chunk
        q_ref[:, lo:lo + col_chunk] = (_dot(xb, w_ref[:, lo:lo + col_chunk]) * (NA_HEAD_DIM ** -0.5)).astype(BF16)
        k_ref[:, lo:lo + col_chunk] = _dot(xb, w_ref[:, d + lo:d + lo + col_chunk]).astype(BF16)
        v_ref[:, lo:lo + col_chunk] = _dot(xb, w_ref[:, 2 * d + lo:2 * d + lo + col_chunk]).astype(BF16)


def _qkv(x2d, w, tm=512, col_chunk=512):
    n, d = x2d.shape
    spec = pl.BlockSpec((tm, d), lambda i: (i, 0))
    shape = jax.ShapeDtypeStruct((n, d), BF16)
    return pl.pallas_call(
        functools.partial(_qkv_kernel, col_chunk=col_chunk),
        grid=(n // tm,),
        in_specs=[spec, _resident(w.shape)],
        out_specs=[spec, spec, spec],
        out_shape=[shape, shape, shape],
        compiler_params=_cparams(1),
        name="na_qkv",
    )(x2d, w)


def _na_core_kernel(q_ref, k_ref, v_ref, bias_ref, o_ref, *, rows):
    r = pl.program_id(1)
    start = pl.multiple_of(jnp.clip(r - NA_WIN_ROWS // 2, 0, rows - NA_WIN_ROWS) * GRID_W, GRID_W)
    keys = pl.ds(start, NA_KEYS)
    pair = 2 * NA_HEAD_DIM
    lo_half = lax.broadcasted_iota(jnp.int32, (1, pair), 1) < NA_HEAD_DIM
    q = q_ref[0, 0]
    for hp in range(NA_HEADS // 2):
        sl = slice(hp * pair, (hp + 1) * pair)
        qp = q[:, sl]
        zero = jnp.zeros_like(qp)
        qm = jnp.concatenate([jnp.where(lo_half, qp, zero), jnp.where(lo_half, zero, qp)], axis=0)
        s = lax.dot_general(qm, k_ref[0, keys, sl], (((1,), (1,)), ((), ())), preferred_element_type=F32)
        s = s + bias_ref[0, 2 * hp:2 * hp + 2].reshape(2 * GRID_W, NA_KEYS)
        p = jnp.exp(s - jnp.max(s, axis=-1, keepdims=True))
        denom = jnp.sum(p, axis=-1, keepdims=True)
        o = _dot(p.astype(BF16), v_ref[0, keys, sl]) / denom
        o_ref[0, 0, :, sl] = jnp.where(lo_half, o[:GRID_W], o[GRID_W:]).astype(BF16)


def _na_core(q, k, v, bias, bsz, seq):
    rows = seq // GRID_W
    d = D_MODEL
    half = NA_WIN_ROWS // 2

    def bias_idx(b, r):
        return (jnp.where(r < half, r, jnp.where(r <= rows - half, half, r - (rows - NA_WIN_ROWS))), 0, 0, 0)

    qspec = pl.BlockSpec((1, 1, GRID_W, d), lambda b, r: (b, r, 0, 0))
    kvspec = pl.BlockSpec((1, seq, d), lambda b, r: (b, 0, 0))
    return pl.pallas_call(
        functools.partial(_na_core_kernel, rows=rows),
        grid=(bsz, rows),
        in_specs=[qspec, kvspec, kvspec, pl.BlockSpec((1, NA_HEADS, GRID_W, NA_KEYS), bias_idx)],
        out_specs=qspec,
        out_shape=jax.ShapeDtypeStruct((bsz, rows, GRID_W, d), BF16),
        compiler_params=_cparams(2),
        name="na_attention",
    )(q.reshape(bsz, rows, GRID_W, d), k.reshape(bsz, seq, d), v.reshape(bsz, seq, d), bias)


def _na_bias_table(rpb):
    i = jnp.arange(NA_WIN_ROWS)
    dr = i[None, :] - i[:, None] + NA_WIN_ROWS - 1
    qc = jnp.arange(GRID_W)
    kc = jnp.arange(GRID_W)
    dc = jnp.clip(kc[None, :] - qc[:, None] + NA_WIN_COLS - 1, 0, 2 * NA_WIN_COLS - 2)
    col_start = jnp.clip(qc - NA_WIN_COLS // 2, 0, GRID_W - NA_WIN_COLS)
    in_win = (kc[None, :] >= col_start[:, None]) & (kc[None, :] < col_start[:, None] + NA_WIN_COLS)
    t = rpb[:, dr[:, None, :, None], dc[None, :, None, :]]
    t = jnp.where(in_win[None, None, :, None, :], t.astype(F32), -jnp.inf)
    return t.transpose(1, 0, 2, 3, 4).reshape(NA_WIN_ROWS, NA_HEADS, GRID_W, NA_KEYS)


def _na_layer(x2d, bsz, seq, w_qkv, rpb, w_out, g, b, alpha):
    n, d = x2d.shape
    q, k, v = _qkv(x2d, w_qkv.astype(BF16))
    o = _na_core(q, k, v, _na_bias_table(rpb), bsz, seq)
    return _proj_ln(o.reshape(n, d), x2d, w_out.astype(BF16), g, b, alpha)


def kernel(x, ln_mix_g, ln_mix_b, ln_ffn_g, ln_ffn_b, ffn_w_in, ffn_w_out, ssd_w_in, ssd_conv_w, ssd_conv_b,
           ssd_dt_bias, ssd_a_log, ssd_d, ssd_norm_w, ssd_w_out, sc_w_in, sc_conv_w, sc_w_out,
           na_w_qkv, na_rpb, na_w_out):
    bsz, seq, d = x.shape
    alpha = (2.0 * DEPTH) ** 0.25
    h = x.reshape(bsz * seq, d)
    for i in range(DEPTH):
        kind = i % N_MIXERS
        j = i // N_MIXERS
        if kind == 0:
            h = _ssd_layer(h, bsz, seq, ssd_w_in[j], ssd_conv_w[j], ssd_conv_b[j], ssd_dt_bias[j], ssd_a_log[j],
                           ssd_d[j], ssd_norm_w[j], ssd_w_out[j], ln_mix_g[i], ln_mix_b[i], alpha)
        elif kind == 1:
            h = _sc_layer(h, seq, sc_w_in[j].astype(BF16), sc_conv_w[j], sc_w_out[j].astype(BF16),
                          ln_mix_g[i], ln_mix_b[i], alpha)
        else:
            h = _na_layer(h, bsz, seq, na_w_qkv[j], na_rpb[j], na_w_out[j], ln_mix_g[i], ln_mix_b[i], alpha)
        h = _mlp_ln(h, ffn_w_in[i].astype(BF16), ffn_w_out[i].astype(BF16), ln_ffn_g[i], ln_ffn_b[i], alpha)
    return h.reshape(bsz, seq, d)
```

```python
import functools

import jax
import jax.numpy as jnp
from jax import lax
from jax.experimental import pallas as pl
from jax.experimental.pallas import tpu as pltpu

F32 = jnp.float32
BF16 = jnp.bfloat16

D_MODEL = 1024
DEPTH = 4
N_MIXERS = 3
FFN_DIM = 4 * D_MODEL
EPS = 1e-5

SSD_D_INNER = 2 * D_MODEL
SSD_HEAD_DIM = 64
SSD_HEADS = SSD_D_INNER // SSD_HEAD_DIM
SSD_GROUPS = 8
SSD_HPG = SSD_HEADS // SSD_GROUPS
SSD_GW = SSD_HPG * SSD_HEAD_DIM
SSD_STATE = 128
SSD_CONV = 5
SSD_CHUNK = 128
SSD_GN = SSD_GROUPS * SSD_STATE
SSD_CONV_DIM = SSD_D_INNER + 2 * SSD_GN
SSD_DT_PAD = 128
SSD_EXP_K = 32

SC_CONV = 3

GRID_W = 64
NA_HEADS = 16
NA_HEAD_DIM = D_MODEL // NA_HEADS
NA_WIN_ROWS = 8
NA_WIN_COLS = 16
NA_KEYS = NA_WIN_ROWS * GRID_W

HALO = 8
VMEM_LIMIT = 56 * 1024 * 1024


def _cparams(n_axes):
    return pltpu.CompilerParams(dimension_semantics=("arbitrary",) * n_axes,
                                vmem_limit_bytes=VMEM_LIMIT)


def _resident(shape):
    nd = len(shape)
    return pl.BlockSpec(shape, lambda *_: (0,) * nd, pipeline_mode=pl.Buffered(1))


def _dot(a, b):
    return jnp.dot(a, b, preferred_element_type=F32)


def _silu(x):
    h = 0.5 * x
    return h + h * jnp.tanh(h)


def _layer_norm(y, g, b):
    mu = jnp.mean(y, axis=-1, keepdims=True)
    yc = y - mu
    var = jnp.mean(yc * yc, axis=-1, keepdims=True)
    return yc * lax.rsqrt(var + EPS) * g + b


def _halo_specs(tm, n_rows, width):
    blocks_per_tile = tm // HALO
    last_block = n_rows // HALO - 1
    prev = pl.BlockSpec((HALO, width), lambda i: (jnp.maximum(i * blocks_per_tile - 1, 0), 0))
    cur = pl.BlockSpec((tm, width), lambda i: (i, 0))
    nxt = pl.BlockSpec((HALO, width), lambda i: (jnp.minimum((i + 1) * blocks_per_tile, last_block), 0))
    return prev, cur, nxt


def _halo_valid(tm, tiles_per_seq):
    i = pl.program_id(0)
    first = (i % tiles_per_seq) == 0
    last = (i % tiles_per_seq) == tiles_per_seq - 1
    rows = lax.broadcasted_iota(jnp.int32, (tm + 2 * HALO, 1), 0)
    ok_top = jnp.logical_or(rows >= HALO, jnp.logical_not(first))
    ok_bot = jnp.logical_or(rows < tm + HALO, jnp.logical_not(last))
    return jnp.logical_and(ok_top, ok_bot)


def _mlp_ln_kernel(x_ref, w1_ref, w2_ref, g_ref, b_ref, o_ref, *, alpha, ff_chunk):
    x = x_ref[...]
    xb = x.astype(BF16)
    acc = alpha * x
    for c in range(FFN_DIM // ff_chunk):
        sl = slice(c * ff_chunk, (c + 1) * ff_chunk)
        h = jnp.maximum(_dot(xb, w1_ref[:, sl]), 0.0)
        acc = acc + _dot((h * h).astype(BF16), w2_ref[sl, :])
    o_ref[...] = _layer_norm(acc, g_ref[...], b_ref[...])


def _mlp_ln(x2d, w1, w2, g, b, alpha, tm=512, ff_chunk=1024):
    n, d = x2d.shape
    return pl.pallas_call(
        functools.partial(_mlp_ln_kernel, alpha=alpha, ff_chunk=ff_chunk),
        grid=(n // tm,),
        in_specs=[pl.BlockSpec((tm, d), lambda i: (i, 0)),
                  _resident(w1.shape), _resident(w2.shape),
                  _resident((1, d)), _resident((1, d))],
        out_specs=pl.BlockSpec((tm, d), lambda i: (i, 0)),
        out_shape=jax.ShapeDtypeStruct((n, d), F32),
        compiler_params=_cparams(1),
        name="mlp_ln",
    )(x2d, w1, w2, g.reshape(1, d), b.reshape(1, d))


def _proj_ln_kernel(y_ref, x_ref, w_ref, g_ref, b_ref, o_ref, *, alpha):
    acc = alpha * x_ref[...] + _dot(y_ref[...], w_ref[...])
    o_ref[...] = _layer_norm(acc, g_ref[...], b_ref[...])


def _proj_ln(y2d, x2d, w, g, b, alpha, tm=512):
    n, d = x2d.shape
    k = y2d.shape[1]
    return pl.pallas_call(
        functools.partial(_proj_ln_kernel, alpha=alpha),
        grid=(n // tm,),
        in_specs=[pl.BlockSpec((tm, k), lambda i: (i, 0)),
                  pl.BlockSpec((tm, d), lambda i: (i, 0)),
                  _resident(w.shape), _resident((1, d)), _resident((1, d))],
        out_specs=pl.BlockSpec((tm, d), lambda i: (i, 0)),
        out_shape=jax.ShapeDtypeStruct((n, d), F32),
        compiler_params=_cparams(1),
        name="proj_ln",
    )(y2d, x2d, w, g.reshape(1, d), b.reshape(1, d))


def _sc_kernel(xp_ref, x_ref, xn_ref, win_ref, cw_ref, wout_ref, g_ref, b_ref, o_ref, v_scr,
               *, tm, tiles_per_seq, alpha, col_chunk):
    d = D_MODEL
    n = tm + 2 * HALO
    x = x_ref[...]
    xcat = jnp.concatenate([xp_ref[...], x, xn_ref[...]], axis=0).astype(BF16)
    xb = xcat[HALO:HALO + tm]
    valid = _halo_valid(tm, tiles_per_seq)
    for j in range(d // col_chunk):
        lo = j * col_chunk
        b_gate = _dot(xb, win_ref[:, lo:lo + col_chunk])
        c_gate = _dot(xcat, win_ref[:, d + lo:d + lo + col_chunk])
        h = _dot(xcat, win_ref[:, 2 * d + lo:2 * d + lo + col_chunk])
        u = jnp.where(valid, c_gate * h, 0.0)
        cw = cw_ref[:, lo:lo + col_chunk]
        conv = (cw[0:1] * pltpu.roll(u, 1, 0) + cw[1:2] * u + cw[2:3] * pltpu.roll(u, n - 1, 0))
        v_scr[:, lo:lo + col_chunk] = (b_gate * conv[HALO:HALO + tm]).astype(BF16)
    acc = alpha * x + _dot(v_scr[...], wout_ref[...])
    o_ref[...] = _layer_norm(acc, g_ref[...], b_ref[...])


def _sc_layer(x2d, seq, w_in, conv_w, w_out, g, b, alpha, tm=512, col_chunk=256):
    n, d = x2d.shape
    prev, cur, nxt = _halo_specs(tm, n, d)
    return pl.pallas_call(
        functools.partial(_sc_kernel, tm=tm, tiles_per_seq=seq // tm, alpha=alpha, col_chunk=col_chunk),
        grid=(n // tm,),
        in_specs=[prev, cur, nxt, _resident(w_in.shape), _resident(conv_w.shape), _resident(w_out.shape),
                  _resident((1, d)), _resident((1, d))],
        out_specs=pl.BlockSpec((tm, d), lambda i: (i, 0)),
        out_shape=jax.ShapeDtypeStruct((n, d), F32),
        scratch_shapes=[pltpu.VMEM((tm, d), BF16)],
        compiler_params=_cparams(1),
        name="short_conv_layer",
    )(x2d, x2d, x2d, w_in, conv_w, w_out, g.reshape(1, d), b.reshape(1, d))


def _ssd_in_kernel(xp_ref, x_ref, xn_ref, wz_ref, wxbc_ref, wdt_ref, cw_ref, cb_ref,
                   z_ref, xs_ref, bt_ref, cm_ref, dt_ref, *, tm, tiles_per_seq, col_chunk):
    n = tm + 2 * HALO
    xcat = jnp.concatenate([xp_ref[...], x_ref[...], xn_ref[...]], axis=0).astype(BF16)
    xb = xcat[HALO:HALO + tm]
    valid = _halo_valid(tm, tiles_per_seq)
    for j in range(SSD_D_INNER // col_chunk):
        lo = j * col_chunk
        z_ref[:, lo:lo + col_chunk] = _dot(xb, wz_ref[:, lo:lo + col_chunk]).astype(BF16)
    dt_ref[...] = _dot(xb, wdt_ref[...])
    for j in range(SSD_CONV_DIM // col_chunk):
        lo = j * col_chunk
        y = jnp.where(valid, _dot(xcat, wxbc_ref[:, lo:lo + col_chunk]), 0.0)
        cw = cw_ref[:, lo:lo + col_chunk]
        conv = (cw[0:1] * pltpu.roll(y, 2, 0) + cw[1:2] * pltpu.roll(y, 1, 0) + cw[2:3] * y
                + cw[3:4] * pltpu.roll(y, n - 1, 0) + cw[4:5] * pltpu.roll(y, n - 2, 0))
        conv = conv[HALO:HALO + tm] + cb_ref[:, lo:lo + col_chunk]
        act = _silu(conv)
        if lo < SSD_D_INNER:
            xs_ref[:, lo:lo + col_chunk] = act.astype(BF16)
        elif lo < SSD_D_INNER + SSD_GN:
            o = lo - SSD_D_INNER
            bt_ref[o:o + col_chunk, :] = act.T.astype(BF16)
        else:
            o = lo - SSD_D_INNER - SSD_GN
            cm_ref[:, o:o + col_chunk] = act.astype(BF16)


def _ssd_in(x2d, seq, wz, wxbc, wdt, conv_w, conv_b, tm=512, col_chunk=512):
    n, d = x2d.shape
    prev, cur, nxt = _halo_specs(tm, n, d)
    row = lambda w: pl.BlockSpec((tm, w), lambda i: (i, 0))
    return pl.pallas_call(
        functools.partial(_ssd_in_kernel, tm=tm, tiles_per_seq=seq // tm, col_chunk=col_chunk),
        grid=(n // tm,),
        in_specs=[prev, cur, nxt, _resident(wz.shape), _resident(wxbc.shape), _resident(wdt.shape),
                  _resident(conv_w.shape), _resident((1, SSD_CONV_DIM))],
        out_specs=[row(SSD_D_INNER), row(SSD_D_INNER), pl.BlockSpec((SSD_GN, tm), lambda i: (0, i)),
                   row(SSD_GN), row(SSD_DT_PAD)],
        out_shape=[jax.ShapeDtypeStruct((n, SSD_D_INNER), BF16), jax.ShapeDtypeStruct((n, SSD_D_INNER), BF16),
                   jax.ShapeDtypeStruct((SSD_GN, n), BF16), jax.ShapeDtypeStruct((n, SSD_GN), BF16),
                   jax.ShapeDtypeStruct((n, SSD_DT_PAD), F32)],
        compiler_params=_cparams(1),
        name="ssd_in_proj_conv",
    )(x2d, x2d, x2d, wz, wxbc, wdt, conv_w, conv_b.reshape(1, SSD_CONV_DIM))


def _split3(v):
    hi = v.astype(BF16)
    rem = v - hi.astype(F32)
    mid = rem.astype(BF16)
    low = (rem - mid.astype(F32)).astype(BF16)
    return jnp.concatenate([hi, mid, low], axis=1)


def _ssd_prep_kernel(dt_ref, bias_ref, acoef_ref, p1_ref, p2_ref, p3_ref, r1_ref, r2_ref, r4_ref, *, tl):
    rows = lax.broadcasted_iota(jnp.int32, (SSD_CHUNK, 1), 0)
    lane = lax.broadcasted_iota(jnp.int32, (1, SSD_DT_PAD), 1)
    is_fwd = (lane // SSD_HPG) % 2 == 0
    for c in range(tl // SSD_CHUNK):
        sl = slice(c * SSD_CHUNK, (c + 1) * SSD_CHUNK)
        raw = dt_ref[0, sl, :] + bias_ref[...]
        dt = jnp.maximum(raw, 0.0) + jnp.log1p(jnp.exp(-jnp.abs(raw)))
        a = dt * acoef_ref[...]
        cs = a
        sh = 1
        while sh < SSD_CHUNK:
            cs = cs + jnp.where(rows >= sh, pltpu.roll(cs, sh, 0), 0.0)
            sh *= 2
        tot = cs[SSD_CHUNK - 1:SSD_CHUNK, :]
        ec = cs - a
        p1 = jnp.where(is_fwd, cs, ec)
        p2 = jnp.where(is_fwd, jnp.exp(cs), jnp.exp(tot - ec))
        p3 = jnp.where(is_fwd, jnp.exp(tot - cs), jnp.exp(ec)) * dt
        etot = jnp.broadcast_to(jnp.exp(tot), (SSD_CHUNK, SSD_DT_PAD))
        p1_ref[0, sl, :] = _split3(p1)
        p2_ref[0, sl, :] = _split3(p2)
        p3_ref[0, sl, :] = _split3(p3)
        log_dt = jnp.log(dt)
        dt_pair = dt + jnp.where(is_fwd, pltpu.roll(dt, SSD_DT_PAD - SSD_HPG, 1), pltpu.roll(dt, SSD_HPG, 1))
        r1_ref[0, :, sl] = jnp.where(is_fwd, cs - log_dt, ec + log_dt).T
        r2_ref[0, :, sl] = jnp.log(dt_pair).T
        r4_ref[0, :, sl] = etot.T


def _ssd_prep(dt_raw, bias, acoef, tl=1024):
    bsz, seq, w = dt_raw.shape
    col_in = pl.BlockSpec((1, tl, w), lambda b, i: (b, i, 0))
    col = pl.BlockSpec((1, tl, 3 * w), lambda b, i: (b, i, 0))
    rowm = pl.BlockSpec((1, w, tl), lambda b, i: (b, 0, i))
    col_shape = jax.ShapeDtypeStruct((bsz, seq, 3 * w), BF16)
    row_shape = jax.ShapeDtypeStruct((bsz, w, seq), F32)
    return pl.pallas_call(
        functools.partial(_ssd_prep_kernel, tl=tl),
        grid=(bsz, seq // tl),
        in_specs=[col_in, pl.BlockSpec((1, w), lambda b, i: (0, 0)), pl.BlockSpec((1, w), lambda b, i: (0, 0))],
        out_specs=[col, col, col, rowm, rowm, rowm],
        out_shape=[col_shape, col_shape, col_shape, row_shape, row_shape, row_shape],
        compiler_params=_cparams(2),
        name="ssd_decay_tables",
    )(dt_raw, bias, acoef)


def _head_lanes(vals, lane):
    out = vals[SSD_HPG - 1]
    for k in range(SSD_HPG - 2, -1, -1):
        out = jnp.where(lane < (k + 1) * SSD_HEAD_DIM, vals[k], out)
    return out


def _ssd_core_kernel(xs_ref, bt_ref, cm_ref, z_ref, p1_ref, p2_ref, p3_ref, r1_ref, r2_ref, r4_ref,
                     dsk_ref, nw_ref, ef_ref, el_ref, o_ref, st_scr, xw_scr, f2_scr, colv_scr, *, nchunks):
    q = SSD_CHUNK
    hpg = SSD_HPG
    lane = lax.broadcasted_iota(jnp.int32, (1, SSD_GW), 1)

    def chunk_decay(c, direction):
        blk = r4_ref[0, 0, :, c * q:(c + 1) * q]
        rows = [jnp.concatenate([blk[direction * hpg + k:direction * hpg + k + 1, :]] * (SSD_GW // q), axis=1)
                for k in range(hpg)]
        return _head_lanes(rows, lane)

    xs_all = xs_ref[0].astype(F32)
    w3 = _dot(p3_ref[0, 0], ef_ref[...])
    xw_scr[...] = (jnp.concatenate([xs_all, xs_all], axis=1) * w3).astype(BF16)
    f2_scr[...] = _dot(p2_ref[0, 0], ef_ref[...])
    colv_scr[...] = _dot(p1_ref[0, 0], el_ref[...])

    def state_body(c, carry):
        rs = pl.ds(pl.multiple_of(c * q, q), q)
        st = _dot(bt_ref[:, rs], xw_scr[rs, :])
        st_scr[c, 0] = st[:, :SSD_GW]
        st_scr[c, 1] = st[:, SSD_GW:]
        return carry

    lax.fori_loop(0, nchunks, state_body, 0, unroll=4)

    s_f = jnp.zeros((SSD_STATE, SSD_GW), F32)
    for c in range(nchunks):
        add = st_scr[c, 0]
        st_scr[c, 0] = s_f
        s_f = chunk_decay(c, 0) * s_f + add
    s_b = jnp.zeros((SSD_STATE, SSD_GW), F32)
    for c in range(nchunks - 1, -1, -1):
        add = st_scr[c, 1]
        st_scr[c, 1] = s_b
        s_b = chunk_decay(c, 1) * s_b + add

    ti = lax.broadcasted_iota(jnp.int32, (q, q), 0)
    si = lax.broadcasted_iota(jnp.int32, (q, q), 1)
    lower_strict = si < ti
    upper_strict = si > ti

    def out_body(c, carry):
        rs = pl.ds(pl.multiple_of(c * q, q), q)
        xs_b = xs_ref[0, rs, :]
        cm = cm_ref[0, rs, :]
        g = _dot(cm, bt_ref[:, rs])
        r1 = r1_ref[0, 0, :, rs]
        r2 = r2_ref[0, 0, :, rs]
        ms = []
        for k in range(hpg):
            kb = hpg + k
            seg = jnp.where(lower_strict, colv_scr[rs, k * q:(k + 1) * q] - r1[k:k + 1, :],
                            jnp.where(upper_strict, r1[kb:kb + 1, :] - colv_scr[rs, kb * q:(kb + 1) * q],
                                      r2[k:k + 1, :]))
            ms.append((g * jnp.exp(seg)).astype(BF16))
        zero = jnp.zeros_like(xs_b)
        xbd = jnp.concatenate(
            [jnp.where(jnp.logical_and(lane >= k * SSD_HEAD_DIM, lane < (k + 1) * SSD_HEAD_DIM), xs_b, zero)
             for k in range(hpg)], axis=0)
        y = _dot(jnp.concatenate(ms, axis=1), xbd)
        s_in = jnp.concatenate([st_scr[c, 0], st_scr[c, 1]], axis=1).astype(BF16)
        y_off = _dot(cm, s_in)
        xs = xs_b.astype(F32)
        f2 = f2_scr[rs, :]
        y = y + f2[:, :SSD_GW] * y_off[:, :SSD_GW] + f2[:, SSD_GW:] * y_off[:, SSD_GW:] + dsk_ref[0] * xs
        z = z_ref[0, rs, :].astype(F32)
        y = y * _silu(z)
        y = y * lax.rsqrt(jnp.mean(y * y, axis=-1, keepdims=True) + EPS)
        o_ref[0, rs, :] = (y * nw_ref[0]).astype(BF16)
        return carry

    lax.fori_loop(0, nchunks, out_body, 0, unroll=4)


def _expansion_matrices():
    nhd = 2 * SSD_HPG
    j = jnp.arange(SSD_EXP_K)[:, None]
    live = j < 3 * nhd
    head_lane = jnp.arange(2 * SSD_GW)[None, :] // SSD_HEAD_DIM
    block_lane = jnp.arange(nhd * SSD_CHUNK)[None, :] // SSD_CHUNK
    ef = jnp.where(live & (j % nhd == head_lane), 1.0, 0.0).astype(BF16)
    el = jnp.where(live & (j % nhd == block_lane), 1.0, 0.0).astype(BF16)
    return ef, el


def _ssd_core(xs, bt, cm, z, p1, p2, p3, r1, r2, r4, dsk, nw):
    bsz, seq, _ = xs.shape
    nchunks = seq // SSD_CHUNK
    ef, el = _expansion_matrices()
    wide = pl.BlockSpec((1, seq, SSD_GW), lambda b, g: (b, 0, g))
    narrow = pl.BlockSpec((1, seq, SSD_STATE), lambda b, g: (b, 0, g))
    state_major = pl.BlockSpec((SSD_STATE, seq), lambda b, g: (g, b))
    colt = pl.BlockSpec((1, 1, seq, SSD_EXP_K), lambda b, g: (b, g, 0, 0))
    rowt = pl.BlockSpec((1, 1, 2 * SSD_HPG, seq), lambda b, g: (b, g, 0, 0))
    pergroup = pl.BlockSpec((1, 1, SSD_GW), lambda b, g: (g, 0, 0))
    return pl.pallas_call(
        functools.partial(_ssd_core_kernel, nchunks=nchunks),
        grid=(bsz, SSD_GROUPS),
        in_specs=[wide, state_major, narrow, wide, colt, colt, colt, rowt, rowt, rowt, pergroup, pergroup,
                  _resident(ef.shape), _resident(el.shape)],
        out_specs=wide,
        out_shape=jax.ShapeDtypeStruct((bsz, seq, SSD_D_INNER), BF16),
        scratch_shapes=[pltpu.VMEM((nchunks, 2, SSD_STATE, SSD_GW), F32),
                        pltpu.VMEM((seq, 2 * SSD_GW), BF16),
                        pltpu.VMEM((seq, 2 * SSD_GW), F32),
                        pltpu.VMEM((seq, 2 * SSD_HPG * SSD_CHUNK), F32)],
        compiler_params=_cparams(2),
        name="ssd_scan",
    )(xs, bt, cm, z, p1, p2, p3, r1, r2, r4, dsk, nw, ef, el)


def _ssd_layer(x2d, bsz, seq, w_in, conv_w, conv_b, dt_bias, a_log, d_skip, norm_w, w_out, g, b, alpha):
    n = x2d.shape[0]
    o_xbc = SSD_D_INNER
    o_dt = SSD_D_INNER + SSD_CONV_DIM
    perm = jnp.array([dr * SSD_HEADS + gi * SSD_HPG + k
                      for gi in range(SSD_GROUPS) for dr in range(2) for k in range(SSD_HPG)], jnp.int32)
    pad = SSD_DT_PAD - 2 * SSD_HEADS
    wz = w_in[:, :o_xbc].astype(BF16)
    wxbc = w_in[:, o_xbc:o_dt].astype(BF16)
    wdt = jnp.pad(w_in[:, o_dt:][:, perm], ((0, 0), (0, pad))).astype(BF16)
    bias = jnp.pad(dt_bias.reshape(-1)[perm], (0, pad)).reshape(1, SSD_DT_PAD)
    acoef = jnp.pad(a_log.reshape(-1)[perm], (0, pad)).reshape(1, SSD_DT_PAD)
    acoef = jnp.where(jnp.arange(SSD_DT_PAD) < 2 * SSD_HEADS, -jnp.exp(acoef), 0.0).reshape(1, SSD_DT_PAD)

    z, xs, bt, cm, dt_raw = _ssd_in(x2d, seq, wz, wxbc, wdt, conv_w, conv_b)
    tables = _ssd_prep(dt_raw.reshape(bsz, seq, SSD_DT_PAD), bias, acoef)
    nhd = 2 * SSD_HPG

    def per_group_cols(t):
        t = t.reshape(bsz, seq, 3, SSD_DT_PAD)[..., :2 * SSD_HEADS].reshape(bsz, seq, 3, SSD_GROUPS, nhd)
        t = t.transpose(0, 3, 1, 2, 4).reshape(bsz, SSD_GROUPS, seq, 3 * nhd)
        return jnp.pad(t, ((0, 0), (0, 0), (0, 0), (0, SSD_EXP_K - 3 * nhd)))

    cols = [per_group_cols(t) for t in tables[:3]]
    rows = [t[:, :2 * SSD_HEADS, :].reshape(bsz, SSD_GROUPS, nhd, seq) for t in tables[3:]]
    dsk = jnp.repeat(d_skip, SSD_HEAD_DIM).reshape(SSD_GROUPS, 1, SSD_GW)
    nw = norm_w.reshape(SSD_GROUPS, 1, SSD_GW)
    y = _ssd_core(xs.reshape(bsz, seq, -1), bt, cm.reshape(bsz, seq, -1),
                  z.reshape(bsz, seq, -1), *cols, *rows, dsk, nw)
    return _proj_ln(y.reshape(n, SSD_D_INNER), x2d, w_out.astype(BF16), g, b, alpha)


def _qkv_kernel(x_ref, w_ref, q_ref, k_ref, v_ref, *, col_chunk):
    d = D_MODEL
    xb = x_ref[...].astype(BF16)
    for j in range(d // col_chunk):
        lo = j * col_chunk
        q_ref[:, lo:lo + col_chunk] = (_dot(xb, w_ref[:, lo:lo + col_chunk]) * (NA_HEAD_DIM ** -0.5)).astype(BF16)
        k_ref[:, lo:lo + col_chunk] = _dot(xb, w_ref[:, d + lo:d + lo + col_chunk]).astype(BF16)
        v_ref[:, lo:lo + col_chunk] = _dot(xb, w_ref[:, 2 * d + lo:2 * d + lo + col_chunk]).astype(BF16)


def _qkv(x2d, w, tm=512, col_chunk=512):
    n, d = x2d.shape
    spec = pl.BlockSpec((tm, d), lambda i: (i, 0))
    shape = jax.ShapeDtypeStruct((n, d), BF16)
    return pl.pallas_call(
        functools.partial(_qkv_kernel, col_chunk=col_chunk),
        grid=(n // tm,),
        in_specs=[spec, _resident(w.shape)],
        out_specs=[spec, spec, spec],
        out_shape=[shape, shape, shape],
        compiler_params=_cparams(1),
        name="na_qkv",
    )(x2d, w)


def _na_core_kernel(q_ref, k_ref, v_ref, bias_ref, o_ref, *, rows):
    r = pl.program_id(1)
    start = pl.multiple_of(jnp.clip(r - NA_WIN_ROWS // 2, 0, rows - NA_WIN_ROWS) * GRID_W, GRID_W)
    keys = pl.ds(start, NA_KEYS)
    pair = 2 * NA_HEAD_DIM
    lo_half = lax.broadcasted_iota(jnp.int32, (1, pair), 1) < NA_HEAD_DIM
    q = q_ref[0, 0]
    for hp in range(NA_HEADS // 2):
        sl = slice(hp * pair, (hp + 1) * pair)
        qp = q[:, sl]
        zero = jnp.zeros_like(qp)
        qm = jnp.concatenate([jnp.where(lo_half, qp, zero), jnp.where(lo_half, zero, qp)], axis=0)
        s = lax.dot_general(qm, k_ref[0, keys, sl], (((1,), (1,)), ((), ())), preferred_element_type=F32)
        s = s + bias_ref[0, 2 * hp:2 * hp + 2].reshape(2 * GRID_W, NA_KEYS)
        p = jnp.exp(s - jnp.max(s, axis=-1, keepdims=True))
        denom = jnp.sum(p, axis=-1, keepdims=True)
        o = _dot(p.astype(BF16), v_ref[0, keys, sl]) / denom
        o_ref[0, 0, :, sl] = jnp.where(lo_half, o[:GRID_W], o[GRID_W:]).astype(BF16)


def _na_core(q, k, v, bias, bsz, seq):
    rows = seq // GRID_W
    d = D_MODEL
    half = NA_WIN_ROWS // 2

    def bias_idx(b, r):
        return (jnp.where(r < half, r, jnp.where(r <= rows - half, half, r - (rows - NA_WIN_ROWS))), 0, 0, 0)

    qspec = pl.BlockSpec((1, 1, GRID_W, d), lambda b, r: (b, r, 0, 0))
    kvspec = pl.BlockSpec((1, seq, d), lambda b, r: (b, 0, 0))
    return pl.pallas_call(
        functools.partial(_na_core_kernel, rows=rows),
        grid=(bsz, rows),
        in_specs=[qspec, kvspec, kvspec, pl.BlockSpec((1, NA_HEADS, GRID_W, NA_KEYS), bias_idx)],
        out_specs=qspec,
        out_shape=jax.ShapeDtypeStruct((bsz, rows, GRID_W, d), BF16),
        compiler_params=_cparams(2),
        name="na_attention",
    )(q.reshape(bsz, rows, GRID_W, d), k.reshape(bsz, seq, d), v.reshape(bsz, seq, d), bias)


def _na_bias_table(rpb):
    w = GRID_W
    n_dr = 2 * NA_WIN_ROWS - 1
    ext = jnp.concatenate([jnp.repeat(rpb[:, :, :1], w - NA_WIN_COLS, axis=2), rpb.astype(F32),
                           jnp.repeat(rpb[:, :, -1:], 2 * w - (w - NA_WIN_COLS) - (2 * NA_WIN_COLS - 1), axis=2)],
                          axis=2)
    flat = jnp.broadcast_to(ext[:, :, None, :], (NA_HEADS, n_dr, w, 2 * w)).reshape(NA_HEADS, n_dr, 2 * w * w)
    toep = flat[:, :, w - 1:w - 1 + w * (2 * w - 1)].reshape(NA_HEADS, n_dr, w, 2 * w - 1)[..., :w]
    qc = jnp.arange(w)
    kc = jnp.arange(w)
    col_start = jnp.clip(qc - NA_WIN_COLS // 2, 0, w - NA_WIN_COLS)
    in_win = (kc[None, :] >= col_start[:, None]) & (kc[None, :] < col_start[:, None] + NA_WIN_COLS)
    toep = jnp.where(in_win[None, None], toep, -jnp.inf)
    per_offset = [toep[:, NA_WIN_ROWS - 1 - o:2 * NA_WIN_ROWS - 1 - o].transpose(0, 2, 1, 3)
                  for o in range(NA_WIN_ROWS)]
    return jnp.stack(per_offset, axis=0).reshape(NA_WIN_ROWS, NA_HEADS, w, NA_KEYS)


def _na_layer(x2d, bsz, seq, w_qkv, rpb, w_out, g, b, alpha):
    n, d = x2d.shape
    q, k, v = _qkv(x2d, w_qkv.astype(BF16))
    o = _na_core(q, k, v, _na_bias_table(rpb), bsz, seq)
    return _proj_ln(o.reshape(n, d), x2d, w_out.astype(BF16), g, b, alpha)


def kernel(x, ln_mix_g, ln_mix_b, ln_ffn_g, ln_ffn_b, ffn_w_in, ffn_w_out, ssd_w_in, ssd_conv_w, ssd_conv_b,
           ssd_dt_bias, ssd_a_log, ssd_d, ssd_norm_w, ssd_w_out, sc_w_in, sc_conv_w, sc_w_out,
           na_w_qkv, na_rpb, na_w_out):
    bsz, seq, d = x.shape
    alpha = (2.0 * DEPTH) ** 0.25
    h = x.reshape(bsz * seq, d)
    for i in range(DEPTH):
        kind = i % N_MIXERS
        j = i // N_MIXERS
        if kind == 0:
            h = _ssd_layer(h, bsz, seq, ssd_w_in[j], ssd_conv_w[j], ssd_conv_b[j], ssd_dt_bias[j], ssd_a_log[j],
                           ssd_d[j], ssd_norm_w[j], ssd_w_out[j], ln_mix_g[i], ln_mix_b[i], alpha)
        elif kind == 1:
            h = _sc_layer(h, seq, sc_w_in[j].astype(BF16), sc_conv_w[j], sc_w_out[j].astype(BF16),
                          ln_mix_g[i], ln_mix_b[i], alpha)
        else:
            h = _na_layer(h, bsz, seq, na_w_qkv[j], na_rpb[j], na_w_out[j], ln_mix_g[i], ln_mix_b[i], alpha)
        h = _mlp_ln(h, ffn_w_in[i].astype(BF16), ffn_w_out[i].astype(BF16), ln_ffn_g[i], ln_ffn_b[i], alpha)
    return h.reshape(bsz, seq, d)
```

```python
import functools

import jax
import jax.numpy as jnp
from jax import lax
from jax.experimental import pallas as pl
from jax.experimental.pallas import tpu as pltpu

F32 = jnp.float32
BF16 = jnp.bfloat16

D_MODEL = 1024
DEPTH = 4
N_MIXERS = 3
FFN_DIM = 4 * D_MODEL
EPS = 1e-5

SSD_D_INNER = 2 * D_MODEL
SSD_HEAD_DIM = 64
SSD_HEADS = SSD_D_INNER // SSD_HEAD_DIM
SSD_GROUPS = 8
SSD_HPG = SSD_HEADS // SSD_GROUPS
SSD_GW = SSD_HPG * SSD_HEAD_DIM
SSD_STATE = 128
SSD_CONV = 5
SSD_CHUNK = 128
SSD_GN = SSD_GROUPS * SSD_STATE
SSD_CONV_DIM = SSD_D_INNER + 2 * SSD_GN
SSD_DT_PAD = 128

SC_CONV = 3

GRID_W = 64
NA_HEADS = 16
NA_HEAD_DIM = D_MODEL // NA_HEADS
NA_WIN_ROWS = 8
NA_WIN_COLS = 16
NA_KEYS = NA_WIN_ROWS * GRID_W
LOG2_E = 1.4426950408889634
NA_Q_SCALE = NA_HEAD_DIM ** -0.5 * LOG2_E

HALO = 8
LANES = 128
VMEM_LIMIT = 56 * 1024 * 1024


def _cparams(n_axes):
    return pltpu.CompilerParams(dimension_semantics=("arbitrary",) * n_axes,
                                vmem_limit_bytes=VMEM_LIMIT)


def _resident(shape):
    nd = len(shape)
    return pl.BlockSpec(shape, lambda *_: (0,) * nd, pipeline_mode=pl.Buffered(1))


def _dot(a, b):
    return jnp.dot(a, b, preferred_element_type=F32)


def _silu(x):
    h = 0.5 * x
    return h + h * jnp.tanh(h)


def _layer_norm(y, g, b):
    mu = jnp.mean(y, axis=-1, keepdims=True)
    yc = y - mu
    var = jnp.mean(yc * yc, axis=-1, keepdims=True)
    return yc * lax.rsqrt(var + EPS) * g + b


def _halo_specs(tm, n_rows, width):
    blocks_per_tile = tm // HALO
    last_block = n_rows // HALO - 1
    prev = pl.BlockSpec((HALO, width), lambda i: (jnp.maximum(i * blocks_per_tile - 1, 0), 0))
    cur = pl.BlockSpec((tm, width), lambda i: (i, 0))
    nxt = pl.BlockSpec((HALO, width), lambda i: (jnp.minimum((i + 1) * blocks_per_tile, last_block), 0))
    return prev, cur, nxt


def _halo_valid(tm, tiles_per_seq):
    i = pl.program_id(0)
    first = (i % tiles_per_seq) == 0
    last = (i % tiles_per_seq) == tiles_per_seq - 1
    rows = lax.broadcasted_iota(jnp.int32, (tm + 2 * HALO, 1), 0)
    ok_top = jnp.logical_or(rows >= HALO, jnp.logical_not(first))
    ok_bot = jnp.logical_or(rows < tm + HALO, jnp.logical_not(last))
    return jnp.logical_and(ok_top, ok_bot)


def _mlp_ln_kernel(x_ref, w1_ref, w2_ref, g_ref, b_ref, o_ref, *, alpha, ff_chunk):
    x = x_ref[...]
    xb = x.astype(BF16)
    acc = alpha * x
    for c in range(FFN_DIM // ff_chunk):
        sl = slice(c * ff_chunk, (c + 1) * ff_chunk)
        h = jnp.maximum(_dot(xb, w1_ref[:, sl]), 0.0)
        acc = acc + _dot((h * h).astype(BF16), w2_ref[sl, :])
    o_ref[...] = _layer_norm(acc, g_ref[...], b_ref[...])


def _mlp_ln(x2d, w1, w2, g, b, alpha, tm=512, ff_chunk=1024):
    n, d = x2d.shape
    return pl.pallas_call(
        functools.partial(_mlp_ln_kernel, alpha=alpha, ff_chunk=ff_chunk),
        grid=(n // tm,),
        in_specs=[pl.BlockSpec((tm, d), lambda i: (i, 0)),
                  _resident(w1.shape), _resident(w2.shape),
                  _resident((1, d)), _resident((1, d))],
        out_specs=pl.BlockSpec((tm, d), lambda i: (i, 0)),
        out_shape=jax.ShapeDtypeStruct((n, d), F32),
        compiler_params=_cparams(1),
        name="mlp_ln",
    )(x2d, w1, w2, g.reshape(1, d), b.reshape(1, d))


def _proj_ln_kernel(y_ref, x_ref, w_ref, g_ref, b_ref, o_ref, *, alpha):
    acc = alpha * x_ref[...] + _dot(y_ref[...], w_ref[...])
    o_ref[...] = _layer_norm(acc, g_ref[...], b_ref[...])


def _proj_ln(y2d, x2d, w, g, b, alpha, tm=512):
    n, d = x2d.shape
    k = y2d.shape[1]
    return pl.pallas_call(
        functools.partial(_proj_ln_kernel, alpha=alpha),
        grid=(n // tm,),
        in_specs=[pl.BlockSpec((tm, k), lambda i: (i, 0)),
                  pl.BlockSpec((tm, d), lambda i: (i, 0)),
                  _resident(w.shape), _resident((1, d)), _resident((1, d))],
        out_specs=pl.BlockSpec((tm, d), lambda i: (i, 0)),
        out_shape=jax.ShapeDtypeStruct((n, d), F32),
        compiler_params=_cparams(1),
        name="proj_ln",
    )(y2d, x2d, w, g.reshape(1, d), b.reshape(1, d))


def _sc_kernel(xp_ref, x_ref, xn_ref, win_ref, cw_ref, wout_ref, g_ref, b_ref, o_ref, v_scr,
               *, tm, tiles_per_seq, alpha, col_chunk):
    d = D_MODEL
    n = tm + 2 * HALO
    x = x_ref[...]
    xcat = jnp.concatenate([xp_ref[...], x, xn_ref[...]], axis=0).astype(BF16)
    xb = xcat[HALO:HALO + tm]
    valid = _halo_valid(tm, tiles_per_seq)
    for j in range(d // col_chunk):
        lo = j * col_chunk
        b_gate = _dot(xb, win_ref[:, lo:lo + col_chunk])
        c_gate = _dot(xcat, win_ref[:, d + lo:d + lo + col_chunk])
        h = _dot(xcat, win_ref[:, 2 * d + lo:2 * d + lo + col_chunk])
        u = jnp.where(valid, c_gate * h, 0.0)
        cw = cw_ref[:, lo:lo + col_chunk]
        conv = (cw[0:1] * pltpu.roll(u, 1, 0) + cw[1:2] * u + cw[2:3] * pltpu.roll(u, n - 1, 0))
        v_scr[:, lo:lo + col_chunk] = (b_gate * conv[HALO:HALO + tm]).astype(BF16)
    acc = alpha * x + _dot(v_scr[...], wout_ref[...])
    o_ref[...] = _layer_norm(acc, g_ref[...], b_ref[...])


def _sc_layer(x2d, seq, w_in, conv_w, w_out, g, b, alpha, tm=512, col_chunk=256):
    n, d = x2d.shape
    prev, cur, nxt = _halo_specs(tm, n, d)
    return pl.pallas_call(
        functools.partial(_sc_kernel, tm=tm, tiles_per_seq=seq // tm, alpha=alpha, col_chunk=col_chunk),
        grid=(n // tm,),
        in_specs=[prev, cur, nxt, _resident(w_in.shape), _resident(conv_w.shape), _resident(w_out.shape),
                  _resident((1, d)), _resident((1, d))],
        out_specs=pl.BlockSpec((tm, d), lambda i: (i, 0)),
        out_shape=jax.ShapeDtypeStruct((n, d), F32),
        scratch_shapes=[pltpu.VMEM((tm, d), BF16)],
        compiler_params=_cparams(1),
        name="short_conv_layer",
    )(x2d, x2d, x2d, w_in, conv_w, w_out, g.reshape(1, d), b.reshape(1, d))


def _ssd_in_kernel(xp_ref, x_ref, xn_ref, wz_ref, wxbc_ref, wdt_ref, cw_ref, cb_ref,
                   z_ref, xs_ref, bt_ref, cm_ref, dt_ref, y_scr, *, tm, tiles_per_seq, col_chunk):
    n = tm + 2 * HALO
    xcat = jnp.concatenate([xp_ref[...], x_ref[...], xn_ref[...]], axis=0).astype(BF16)
    xb = xcat[HALO:HALO + tm]
    valid = _halo_valid(tm, tiles_per_seq)
    n_conv = SSD_CONV_DIM // col_chunk
    n_gate = SSD_D_INNER // col_chunk
    for j in range(n_conv):
        if j % (n_conv // n_gate) == 0:
            glo = (j // (n_conv // n_gate)) * col_chunk
            z_ref[:, glo:glo + col_chunk] = _dot(xb, wz_ref[:, glo:glo + col_chunk]).astype(BF16)
        if j == n_conv - 1:
            dt_ref[...] = _dot(xb, wdt_ref[...])
        lo = j * col_chunk
        y = _dot(xcat, wxbc_ref[:, lo:lo + col_chunk])
        y = jnp.concatenate([jnp.where(valid[:HALO], y[:HALO], 0.0), y[HALO:HALO + tm],
                             jnp.where(valid[HALO + tm:], y[HALO + tm:], 0.0)], axis=0)
        half = SSD_CONV // 2
        convs = []
        for s in range(col_chunk // LANES):
            yb = y_scr.at[j % 2, s]
            yb[...] = y[:, s * LANES:(s + 1) * LANES]
            cw = cw_ref[:, lo + s * LANES:lo + (s + 1) * LANES]
            conv = cb_ref[:, lo + s * LANES:lo + (s + 1) * LANES]
            for k in range(SSD_CONV):
                conv = conv + cw[k:k + 1] * yb[HALO + k - half:HALO + k - half + tm, :]
            convs.append(conv)
        act = _silu(jnp.concatenate(convs, axis=1))
        if lo < SSD_D_INNER:
            xs_ref[:, lo:lo + col_chunk] = act.astype(BF16)
        elif lo < SSD_D_INNER + SSD_GN:
            o = lo - SSD_D_INNER
            bt_ref[o:o + col_chunk, :] = act.T.astype(BF16)
        else:
            o = lo - SSD_D_INNER - SSD_GN
            cm_ref[:, o:o + col_chunk] = act.astype(BF16)


def _ssd_in(x2d, seq, wz, wxbc, wdt, conv_w, conv_b, tm=512, col_chunk=512):
    n, d = x2d.shape
    prev, cur, nxt = _halo_specs(tm, n, d)
    row = lambda w: pl.BlockSpec((tm, w), lambda i: (i, 0))
    return pl.pallas_call(
        functools.partial(_ssd_in_kernel, tm=tm, tiles_per_seq=seq // tm, col_chunk=col_chunk),
        grid=(n // tm,),
        in_specs=[prev, cur, nxt, _resident(wz.shape), _resident(wxbc.shape), _resident(wdt.shape),
                  _resident(conv_w.shape), _resident((1, SSD_CONV_DIM))],
        out_specs=[row(SSD_D_INNER), row(SSD_D_INNER), pl.BlockSpec((SSD_GN, tm), lambda i: (0, i)),
                   row(SSD_GN), row(SSD_DT_PAD)],
        out_shape=[jax.ShapeDtypeStruct((n, SSD_D_INNER), BF16), jax.ShapeDtypeStruct((n, SSD_D_INNER), BF16),
                   jax.ShapeDtypeStruct((SSD_GN, n), BF16), jax.ShapeDtypeStruct((n, SSD_GN), BF16),
                   jax.ShapeDtypeStruct((n, SSD_DT_PAD), F32)],
        scratch_shapes=[pltpu.VMEM((2, col_chunk // LANES, tm + 2 * HALO, LANES), F32)],
        compiler_params=_cparams(1),
        name="ssd_in_proj_conv",
    )(x2d, x2d, x2d, wz, wxbc, wdt, conv_w, conv_b.reshape(1, SSD_CONV_DIM))


def _split3(v):
    nlive = 2 * SSD_HEADS
    live = lax.broadcasted_iota(jnp.int32, (1, SSD_DT_PAD), 1) < nlive
    hi = v.astype(BF16)
    rem = v - hi.astype(F32)
    rem2 = rem - rem.astype(BF16).astype(F32)
    first = jnp.where(live, v, pltpu.roll(rem, nlive, 1)).astype(BF16)
    second = jnp.where(live, rem2, 0.0).astype(BF16)
    return jnp.concatenate([first, second], axis=1)


def _ssd_prep_kernel(dt_ref, bias_ref, acoef_ref, p1_ref, p2_ref, p3_ref, r1_ref, r2_ref, r4_ref, *, tl):
    rows = lax.broadcasted_iota(jnp.int32, (SSD_CHUNK, 1), 0)
    lane = lax.broadcasted_iota(jnp.int32, (1, SSD_DT_PAD), 1)
    is_fwd = (lane // SSD_HPG) % 2 == 0
    for c in range(tl // SSD_CHUNK):
        sl = slice(c * SSD_CHUNK, (c + 1) * SSD_CHUNK)
        raw = dt_ref[0, sl, :] + bias_ref[...]
        dt = jnp.maximum(raw, 0.0) + jnp.log1p(jnp.exp(-jnp.abs(raw)))
        a = dt * acoef_ref[...]
        cs = a
        sh = 1
        while sh < SSD_CHUNK:
            cs = cs + jnp.where(rows >= sh, pltpu.roll(cs, sh, 0), 0.0)
            sh *= 2
        tot = cs[SSD_CHUNK - 1:SSD_CHUNK, :]
        ec = cs - a
        p1 = jnp.where(is_fwd, cs, ec)
        p2 = jnp.where(is_fwd, jnp.exp(cs), jnp.exp(tot - ec))
        p3 = jnp.where(is_fwd, jnp.exp(tot - cs), jnp.exp(ec)) * dt
        etot = jnp.broadcast_to(jnp.exp(tot), (SSD_CHUNK, SSD_DT_PAD))
        p1_ref[0, sl, :] = _split3(p1)
        p2_ref[0, sl, :] = _split3(p2)
        p3_ref[0, sl, :] = _split3(p3)
        log_dt = jnp.log(dt)
        dt_pair = dt + jnp.where(is_fwd, pltpu.roll(dt, SSD_DT_PAD - SSD_HPG, 1), pltpu.roll(dt, SSD_HPG, 1))
        r1_ref[0, :, sl] = jnp.where(is_fwd, cs - log_dt, ec + log_dt).T
        r2_ref[0, :, sl] = jnp.log(dt_pair).T
        r4_ref[0, :, sl] = etot.T


def _ssd_prep(dt_raw, bias, acoef, tl=1024):
    bsz, seq, w = dt_raw.shape
    col_in = pl.BlockSpec((1, tl, w), lambda b, i: (b, i, 0))
    col = pl.BlockSpec((1, tl, 2 * w), lambda b, i: (b, i, 0))
    rowm = pl.BlockSpec((1, w, tl), lambda b, i: (b, 0, i))
    col_shape = jax.ShapeDtypeStruct((bsz, seq, 2 * w), BF16)
    row_shape = jax.ShapeDtypeStruct((bsz, w, seq), F32)
    return pl.pallas_call(
        functools.partial(_ssd_prep_kernel, tl=tl),
        grid=(bsz, seq // tl),
        in_specs=[col_in, pl.BlockSpec((1, w), lambda b, i: (0, 0)), pl.BlockSpec((1, w), lambda b, i: (0, 0))],
        out_specs=[col, col, col, rowm, rowm, rowm],
        out_shape=[col_shape, col_shape, col_shape, row_shape, row_shape, row_shape],
        compiler_params=_cparams(2),
        name="ssd_decay_tables",
    )(dt_raw, bias, acoef)


def _head_lanes(vals, lane):
    out = vals[SSD_HPG - 1]
    for k in range(SSD_HPG - 2, -1, -1):
        out = jnp.where(lane < (k + 1) * SSD_HEAD_DIM, vals[k], out)
    return out


def _ssd_core_kernel(xs_ref, bt_ref, cm_ref, z_ref, p1_ref, p2_ref, p3_ref, r1_ref, r2_ref, r4_ref,
                     dsk_ref, nw_ref, ef_ref, el_ref, o_ref, st_scr, xw_scr, f2_scr, colv_scr, *, nchunks):
    q = SSD_CHUNK
    hpg = SSD_HPG
    lane = lax.broadcasted_iota(jnp.int32, (1, SSD_GW), 1)

    def chunk_decay(c, direction):
        blk = r4_ref[0, :, c * q:(c + 1) * q]
        rows = [jnp.concatenate([blk[direction * hpg + k:direction * hpg + k + 1, :]] * (SSD_GW // q), axis=1)
                for k in range(hpg)]
        return _head_lanes(rows, lane)

    xs_all = xs_ref[0].astype(F32)
    w3 = _dot(p3_ref[0], ef_ref[0])
    xw_scr[...] = (jnp.concatenate([xs_all, xs_all], axis=1) * w3).astype(BF16)
    f2_scr[...] = _dot(p2_ref[0], ef_ref[0])
    colv_scr[...] = _dot(p1_ref[0], el_ref[0])

    def state_body(c, carry):
        rs = pl.ds(pl.multiple_of(c * q, q), q)
        st = _dot(bt_ref[:, rs], xw_scr[rs, :])
        st_scr[c, 0] = st[:, :SSD_GW]
        st_scr[c, 1] = st[:, SSD_GW:]
        return carry

    lax.fori_loop(0, nchunks, state_body, 0, unroll=4)

    s_f = jnp.zeros((SSD_STATE, SSD_GW), F32)
    for c in range(nchunks):
        add = st_scr[c, 0]
        st_scr[c, 0] = s_f
        s_f = chunk_decay(c, 0) * s_f + add
    s_b = jnp.zeros((SSD_STATE, SSD_GW), F32)
    for c in range(nchunks - 1, -1, -1):
        add = st_scr[c, 1]
        st_scr[c, 1] = s_b
        s_b = chunk_decay(c, 1) * s_b + add

    ti = lax.broadcasted_iota(jnp.int32, (q, q), 0)
    si = lax.broadcasted_iota(jnp.int32, (q, q), 1)
    lower_strict = si < ti
    upper_strict = si > ti

    def out_body(c, carry):
        rs = pl.ds(pl.multiple_of(c * q, q), q)
        xs_b = xs_ref[0, rs, :]
        cm = cm_ref[0, rs, :]
        g = _dot(cm, bt_ref[:, rs])
        r1 = r1_ref[0, :, rs]
        r2 = r2_ref[0, :, rs]
        ms = []
        for k in range(hpg):
            kb = hpg + k
            seg = jnp.where(lower_strict, colv_scr[rs, k * q:(k + 1) * q] - r1[k:k + 1, :],
                            jnp.where(upper_strict, r1[kb:kb + 1, :] - colv_scr[rs, kb * q:(kb + 1) * q],
                                      r2[k:k + 1, :]))
            ms.append((g * jnp.exp(seg)).astype(BF16))
        zero = jnp.zeros_like(xs_b)
        xbd = jnp.concatenate(
            [jnp.where(jnp.logical_and(lane >= k * SSD_HEAD_DIM, lane < (k + 1) * SSD_HEAD_DIM), xs_b, zero)
             for k in range(hpg)], axis=0)
        y = _dot(jnp.concatenate(ms, axis=1), xbd)
        s_in = jnp.concatenate([st_scr[c, 0], st_scr[c, 1]], axis=1).astype(BF16)
        y_off = _dot(cm, s_in)
        xs = xs_b.astype(F32)
        f2 = f2_scr[rs, :]
        y = y + f2[:, :SSD_GW] * y_off[:, :SSD_GW] + f2[:, SSD_GW:] * y_off[:, SSD_GW:] + dsk_ref[0] * xs
        z = z_ref[0, rs, :].astype(F32)
        y = y * _silu(z)
        y = y * lax.rsqrt(jnp.mean(y * y, axis=-1, keepdims=True) + EPS)
        o_ref[0, rs, :] = (y * nw_ref[0]).astype(BF16)
        return carry

    lax.fori_loop(0, nchunks, out_body, 0, unroll=4)


def _expansion_matrices():
    nhd = 2 * SSD_HPG
    nlive = 2 * SSD_HEADS
    row = jnp.arange(2 * SSD_DT_PAD)[None, :, None]
    grp = jnp.arange(SSD_GROUPS)[:, None, None]
    col_of_row = row % nlive - grp * nhd
    live = row < 3 * nlive
    head_lane = jnp.arange(2 * SSD_GW)[None, None, :] // SSD_HEAD_DIM
    block_lane = jnp.arange(nhd * SSD_CHUNK)[None, None, :] // SSD_CHUNK
    ef = jnp.where(live & (col_of_row == head_lane), 1.0, 0.0).astype(BF16)
    el = jnp.where(live & (col_of_row == block_lane), 1.0, 0.0).astype(BF16)
    return ef, el


def _ssd_core(xs, bt, cm, z, p1, p2, p3, r1, r2, r4, dsk, nw):
    bsz, seq, _ = xs.shape
    nchunks = seq // SSD_CHUNK
    ef, el = _expansion_matrices()
    wide = pl.BlockSpec((1, seq, SSD_GW), lambda b, g: (b, 0, g))
    narrow = pl.BlockSpec((1, seq, SSD_STATE), lambda b, g: (b, 0, g))
    state_major = pl.BlockSpec((SSD_STATE, seq), lambda b, g: (g, b))
    colt = pl.BlockSpec((1, seq, 2 * SSD_DT_PAD), lambda b, g: (b, 0, 0))
    rowt = pl.BlockSpec((1, 2 * SSD_HPG, seq), lambda b, g: (b, g, 0))
    pergroup = pl.BlockSpec((1, 1, SSD_GW), lambda b, g: (g, 0, 0))
    return pl.pallas_call(
        functools.partial(_ssd_core_kernel, nchunks=nchunks),
        grid=(bsz, SSD_GROUPS),
        in_specs=[wide, state_major, narrow, wide, colt, colt, colt, rowt, rowt, rowt, pergroup, pergroup,
                  pl.BlockSpec((1,) + ef.shape[1:], lambda b, g: (g, 0, 0)),
                  pl.BlockSpec((1,) + el.shape[1:], lambda b, g: (g, 0, 0))],
        out_specs=wide,
        out_shape=jax.ShapeDtypeStruct((bsz, seq, SSD_D_INNER), BF16),
        scratch_shapes=[pltpu.VMEM((nchunks, 2, SSD_STATE, SSD_GW), F32),
                        pltpu.VMEM((seq, 2 * SSD_GW), BF16),
                        pltpu.VMEM((seq, 2 * SSD_GW), F32),
                        pltpu.VMEM((seq, 2 * SSD_HPG * SSD_CHUNK), F32)],
        compiler_params=_cparams(2),
        name="ssd_scan",
    )(xs, bt, cm, z, p1, p2, p3, r1, r2, r4, dsk, nw, ef, el)


def _ssd_layer(x2d, bsz, seq, w_in, conv_w, conv_b, dt_bias, a_log, d_skip, norm_w, w_out, g, b, alpha):
    n = x2d.shape[0]
    o_xbc = SSD_D_INNER
    o_dt = SSD_D_INNER + SSD_CONV_DIM
    perm = jnp.array([dr * SSD_HEADS + gi * SSD_HPG + k
                      for gi in range(SSD_GROUPS) for dr in range(2) for k in range(SSD_HPG)], jnp.int32)
    pad = SSD_DT_PAD - 2 * SSD_HEADS
    wz = w_in[:, :o_xbc].astype(BF16)
    wxbc = w_in[:, o_xbc:o_dt].astype(BF16)
    wdt = jnp.pad(w_in[:, o_dt:][:, perm], ((0, 0), (0, pad))).astype(BF16)
    bias = jnp.pad(dt_bias.reshape(-1)[perm], (0, pad)).reshape(1, SSD_DT_PAD)
    acoef = jnp.pad(a_log.reshape(-1)[perm], (0, pad)).reshape(1, SSD_DT_PAD)
    acoef = jnp.where(jnp.arange(SSD_DT_PAD) < 2 * SSD_HEADS, -jnp.exp(acoef), 0.0).reshape(1, SSD_DT_PAD)

    z, xs, bt, cm, dt_raw = _ssd_in(x2d, seq, wz, wxbc, wdt, conv_w, conv_b)
    tables = _ssd_prep(dt_raw.reshape(bsz, seq, SSD_DT_PAD), bias, acoef)
    dsk = jnp.repeat(d_skip, SSD_HEAD_DIM).reshape(SSD_GROUPS, 1, SSD_GW)
    nw = norm_w.reshape(SSD_GROUPS, 1, SSD_GW)
    y = _ssd_core(xs.reshape(bsz, seq, -1), bt, cm.reshape(bsz, seq, -1),
                  z.reshape(bsz, seq, -1), *tables, dsk, nw)
    return _proj_ln(y.reshape(n, SSD_D_INNER), x2d, w_out.astype(BF16), g, b, alpha)


def _qkv_kernel(x_ref, w_ref, q_ref, k_ref, v_ref, *, col_chunk):
    d = D_MODEL
    xb = x_ref[...].astype(BF16)
    for j in range(d // col_chunk):
        lo = j * col_chunk
        q_ref[:, lo:lo + col_chunk] = (_dot(xb, w_ref[:, lo:lo + col_chunk]) * NA_Q_SCALE).astype(BF16)
        k_ref[:, lo:lo + col_chunk] = _dot(xb, w_ref[:, d + lo:d + lo + col_chunk]).astype(BF16)
        v_ref[:, lo:lo + col_chunk] = _dot(xb, w_ref[:, 2 * d + lo:2 * d + lo + col_chunk]).astype(BF16)


def _qkv(x2d, w, tm=512, col_chunk=512):
    n, d = x2d.shape
    spec = pl.BlockSpec((tm, d), lambda i: (i, 0))
    shape = jax.ShapeDtypeStruct((n, d), BF16)
    return pl.pallas_call(
        functools.partial(_qkv_kernel, col_chunk=col_chunk),
        grid=(n // tm,),
        in_specs=[spec, _resident(w.shape)],
        out_specs=[spec, spec, spec],
        out_shape=[shape, shape, shape],
        compiler_params=_cparams(1),
        name="na_qkv",
    )(x2d, w)


def _na_core_kernel(q_ref, k_ref, v_ref, bias_ref, o_ref, s_scr, p_scr, *, rows):
    r = pl.program_id(1)
    start = pl.multiple_of(jnp.clip(r - NA_WIN_ROWS // 2, 0, rows - NA_WIN_ROWS) * GRID_W, GRID_W)
    keys = pl.ds(start, NA_KEYS)
    pair = 2 * NA_HEAD_DIM
    n_pairs = NA_HEADS // 2
    lo_half = lax.broadcasted_iota(jnp.int32, (1, pair), 1) < NA_HEAD_DIM
    q = q_ref[0, 0]
    for hp in range(n_pairs):
        sl = slice(hp * pair, (hp + 1) * pair)
        qp = q[:, sl]
        zero = jnp.zeros_like(qp)
        qm = jnp.concatenate([jnp.where(lo_half, qp, zero), jnp.where(lo_half, zero, qp)], axis=0)
        s = lax.dot_general(qm, k_ref[0, keys, sl], (((1,), (1,)), ((), ())), preferred_element_type=F32)
        s_scr[hp] = s + bias_ref[0, 2 * hp:2 * hp + 2].reshape(2 * GRID_W, NA_KEYS)
    denoms = []
    for hp in range(n_pairs):
        s = s_scr[hp]
        p = jnp.exp2(s - jnp.max(s, axis=-1, keepdims=True))
        denoms.append(jnp.sum(p, axis=-1, keepdims=True))
        p_scr[hp] = p.astype(BF16)
    for hp in range(n_pairs):
        sl = slice(hp * pair, (hp + 1) * pair)
        o = _dot(p_scr[hp], v_ref[0, keys, sl]) / denoms[hp]
        o_ref[0, 0, :, sl] = jnp.where(lo_half, o[:GRID_W], o[GRID_W:]).astype(BF16)


def _na_core(q, k, v, bias, bsz, seq):
    rows = seq // GRID_W
    d = D_MODEL
    half = NA_WIN_ROWS // 2

    def bias_idx(b, r):
        return (jnp.where(r < half, r, jnp.where(r <= rows - half, half, r - (rows - NA_WIN_ROWS))), 0, 0, 0)

    qspec = pl.BlockSpec((1, 1, GRID_W, d), lambda b, r: (b, r, 0, 0))
    kvspec = pl.BlockSpec((1, seq, d), lambda b, r: (b, 0, 0))
    return pl.pallas_call(
        functools.partial(_na_core_kernel, rows=rows),
        grid=(bsz, rows),
        in_specs=[qspec, kvspec, kvspec, pl.BlockSpec((1, NA_HEADS, GRID_W, NA_KEYS), bias_idx)],
        out_specs=qspec,
        out_shape=jax.ShapeDtypeStruct((bsz, rows, GRID_W, d), BF16),
        scratch_shapes=[pltpu.VMEM((NA_HEADS // 2, 2 * GRID_W, NA_KEYS), F32),
                        pltpu.VMEM((NA_HEADS // 2, 2 * GRID_W, NA_KEYS), BF16)],
        compiler_params=_cparams(2),
        name="na_attention",
    )(q.reshape(bsz, rows, GRID_W, d), k.reshape(bsz, seq, d), v.reshape(bsz, seq, d), bias)


def _na_bias_table(rpb):
    w = GRID_W
    n_dr = 2 * NA_WIN_ROWS - 1
    ext = jnp.concatenate([jnp.repeat(rpb[:, :, :1], w - NA_WIN_COLS, axis=2), rpb.astype(F32),
                           jnp.repeat(rpb[:, :, -1:], 2 * w - (w - NA_WIN_COLS) - (2 * NA_WIN_COLS - 1), axis=2)],
                          axis=2)
    flat = jnp.broadcast_to(ext[:, :, None, :], (NA_HEADS, n_dr, w, 2 * w)).reshape(NA_HEADS, n_dr, 2 * w * w)
    toep = flat[:, :, w - 1:w - 1 + w * (2 * w - 1)].reshape(NA_HEADS, n_dr, w, 2 * w - 1)[..., :w]
    qc = jnp.arange(w)
    kc = jnp.arange(w)
    col_start = jnp.clip(qc - NA_WIN_COLS // 2, 0, w - NA_WIN_COLS)
    in_win = (kc[None, :] >= col_start[:, None]) & (kc[None, :] < col_start[:, None] + NA_WIN_COLS)
    toep = jnp.where(in_win[None, None], toep, -jnp.inf)
    per_offset = [toep[:, NA_WIN_ROWS - 1 - o:2 * NA_WIN_ROWS - 1 - o].transpose(0, 2, 1, 3)
                  for o in range(NA_WIN_ROWS)]
    return jnp.stack(per_offset, axis=0).reshape(NA_WIN_ROWS, NA_HEADS, w, NA_KEYS) * LOG2_E


def _na_layer(x2d, bsz, seq, w_qkv, rpb, w_out, g, b, alpha):
    n, d = x2d.shape
    q, k, v = _qkv(x2d, w_qkv.astype(BF16))
    o = _na_core(q, k, v, _na_bias_table(rpb), bsz, seq)
    return _proj_ln(o.reshape(n, d), x2d, w_out.astype(BF16), g, b, alpha)


def kernel(x, ln_mix_g, ln_mix_b, ln_ffn_g, ln_ffn_b, ffn_w_in, ffn_w_out, ssd_w_in, ssd_conv_w, ssd_conv_b,
           ssd_dt_bias, ssd_a_log, ssd_d, ssd_norm_w, ssd_w_out, sc_w_in, sc_conv_w, sc_w_out,
           na_w_qkv, na_rpb, na_w_out):
    bsz, seq, d = x.shape
    alpha = (2.0 * DEPTH) ** 0.25
    h = x.reshape(bsz * seq, d)
    for i in range(DEPTH):
        kind = i % N_MIXERS
        j = i // N_MIXERS
        if kind == 0:
            h = _ssd_layer(h, bsz, seq, ssd_w_in[j], ssd_conv_w[j], ssd_conv_b[j], ssd_dt_bias[j], ssd_a_log[j],
                           ssd_d[j], ssd_norm_w[j], ssd_w_out[j], ln_mix_g[i], ln_mix_b[i], alpha)
        elif kind == 1:
            h = _sc_layer(h, seq, sc_w_in[j].astype(BF16), sc_conv_w[j], sc_w_out[j].astype(BF16),
                          ln_mix_g[i], ln_mix_b[i], alpha)
        else:
            h = _na_layer(h, bsz, seq, na_w_qkv[j], na_rpb[j], na_w_out[j], ln_mix_g[i], ln_mix_b[i], alpha)
        h = _mlp_ln(h, ffn_w_in[i].astype(BF16), ffn_w_out[i].astype(BF16), ln_ffn_g[i], ln_ffn_b[i], alpha)
    return h.reshape(bsz, seq, d)
```

```python
import functools

import jax
import jax.numpy as jnp
from jax import lax
from jax.experimental import pallas as pl
from jax.experimental.pallas import tpu as pltpu

F32 = jnp.float32
BF16 = jnp.bfloat16

D_MODEL = 1024
DEPTH = 4
N_MIXERS = 3
FFN_DIM = 4 * D_MODEL
EPS = 1e-5

SSD_D_INNER = 2 * D_MODEL
SSD_HEAD_DIM = 64
SSD_HEADS = SSD_D_INNER // SSD_HEAD_DIM
SSD_GROUPS = 8
SSD_HPG = SSD_HEADS // SSD_GROUPS
SSD_GW = SSD_HPG * SSD_HEAD_DIM
SSD_STATE = 128
SSD_CONV = 5
SSD_CHUNK = 128
SSD_GN = SSD_GROUPS * SSD_STATE
SSD_CONV_DIM = SSD_D_INNER + 2 * SSD_GN
SSD_DT_PAD = 128
SSD_SCAN_SEGMENTS = 4

SC_CONV = 3

GRID_W = 64
NA_HEADS = 16
NA_HEAD_DIM = D_MODEL // NA_HEADS
NA_WIN_ROWS = 8
NA_WIN_COLS = 16
NA_KEYS = NA_WIN_ROWS * GRID_W
LOG2_E = 1.4426950408889634
NA_Q_SCALE = NA_HEAD_DIM ** -0.5 * LOG2_E

HALO = 8
LANES = 128
SSD_IN_YBUFS = 2
VMEM_LIMIT = 56 * 1024 * 1024


def _cparams(n_axes):
    return pltpu.CompilerParams(dimension_semantics=("arbitrary",) * n_axes,
                                vmem_limit_bytes=VMEM_LIMIT)


def _resident(shape):
    nd = len(shape)
    return pl.BlockSpec(shape, lambda *_: (0,) * nd, pipeline_mode=pl.Buffered(1))


def _dot(a, b):
    return jnp.dot(a, b, preferred_element_type=F32)


def _silu(x):
    h = 0.5 * x
    return h + h * jnp.tanh(h)


def _layer_norm(y, g, b):
    mu = jnp.mean(y, axis=-1, keepdims=True)
    yc = y - mu
    var = jnp.mean(yc * yc, axis=-1, keepdims=True)
    return yc * lax.rsqrt(var + EPS) * g + b


def _halo_specs(tm, n_rows, width):
    blocks_per_tile = tm // HALO
    last_block = n_rows // HALO - 1
    prev = pl.BlockSpec((HALO, width), lambda i: (jnp.maximum(i * blocks_per_tile - 1, 0), 0))
    cur = pl.BlockSpec((tm, width), lambda i: (i, 0))
    nxt = pl.BlockSpec((HALO, width), lambda i: (jnp.minimum((i + 1) * blocks_per_tile, last_block), 0))
    return prev, cur, nxt


def _halo_valid(tm, tiles_per_seq):
    i = pl.program_id(0)
    first = (i % tiles_per_seq) == 0
    last = (i % tiles_per_seq) == tiles_per_seq - 1
    rows = lax.broadcasted_iota(jnp.int32, (tm + 2 * HALO, 1), 0)
    ok_top = jnp.logical_or(rows >= HALO, jnp.logical_not(first))
    ok_bot = jnp.logical_or(rows < tm + HALO, jnp.logical_not(last))
    return jnp.logical_and(ok_top, ok_bot)


FF_CHUNK = 1024


def _mlp_ln_tail(x1, w1_ref, w2_ref, g_ref, b_ref, alpha):
    xb = x1.astype(BF16)
    acc = alpha * x1
    for c in range(FFN_DIM // FF_CHUNK):
        sl = slice(c * FF_CHUNK, (c + 1) * FF_CHUNK)
        h = jnp.maximum(_dot(xb, w1_ref[:, sl]), 0.0)
        acc = acc + _dot((h * h).astype(BF16), w2_ref[sl, :])
    return _layer_norm(acc, g_ref[...], b_ref[...])


def _ffn_operands(ffn, d):
    w1, w2, g, b = ffn
    return ([w1, w2, g.reshape(1, d), b.reshape(1, d)],
            [_resident(w1.shape), _resident(w2.shape), _resident((1, d)), _resident((1, d))])


def _proj_mlp_kernel(y_ref, x_ref, w_ref, g_ref, b_ref, w1_ref, w2_ref, g2_ref, b2_ref, o_ref, *, alpha):
    x1 = _layer_norm(alpha * x_ref[...] + _dot(y_ref[...], w_ref[...]), g_ref[...], b_ref[...])
    o_ref[...] = _mlp_ln_tail(x1, w1_ref, w2_ref, g2_ref, b2_ref, alpha)


def _proj_mlp(y2d, x2d, w, g, b, ffn, alpha, tm=512):
    n, d = x2d.shape
    k = y2d.shape[1]
    ffn_args, ffn_specs = _ffn_operands(ffn, d)
    return pl.pallas_call(
        functools.partial(_proj_mlp_kernel, alpha=alpha),
        grid=(n // tm,),
        in_specs=[pl.BlockSpec((tm, k), lambda i: (i, 0)),
                  pl.BlockSpec((tm, d), lambda i: (i, 0)),
                  _resident(w.shape), _resident((1, d)), _resident((1, d))] + ffn_specs,
        out_specs=pl.BlockSpec((tm, d), lambda i: (i, 0)),
        out_shape=jax.ShapeDtypeStruct((n, d), F32),
        compiler_params=_cparams(1),
        name="proj_ln_mlp_ln",
    )(y2d, x2d, w, g.reshape(1, d), b.reshape(1, d), *ffn_args)


def _sc_kernel(xp_ref, x_ref, xn_ref, win_ref, cw_ref, wout_ref, g_ref, b_ref, w1_ref, w2_ref, g2_ref, b2_ref,
               o_ref, v_scr, *, tm, tiles_per_seq, alpha, col_chunk):
    d = D_MODEL
    n = tm + 2 * HALO
    x = x_ref[...]
    xcat = jnp.concatenate([xp_ref[...], x, xn_ref[...]], axis=0).astype(BF16)
    xb = xcat[HALO:HALO + tm]
    valid = _halo_valid(tm, tiles_per_seq)
    for j in range(d // col_chunk):
        lo = j * col_chunk
        b_gate = _dot(xb, win_ref[:, lo:lo + col_chunk])
        c_gate = _dot(xcat, win_ref[:, d + lo:d + lo + col_chunk])
        h = _dot(xcat, win_ref[:, 2 * d + lo:2 * d + lo + col_chunk])
        u = jnp.where(valid, c_gate * h, 0.0)
        cw = cw_ref[:, lo:lo + col_chunk]
        conv = (cw[0:1] * pltpu.roll(u, 1, 0) + cw[1:2] * u + cw[2:3] * pltpu.roll(u, n - 1, 0))
        v_scr[:, lo:lo + col_chunk] = (b_gate * conv[HALO:HALO + tm]).astype(BF16)
    x1 = _layer_norm(alpha * x + _dot(v_scr[...], wout_ref[...]), g_ref[...], b_ref[...])
    o_ref[...] = _mlp_ln_tail(x1, w1_ref, w2_ref, g2_ref, b2_ref, alpha)


def _sc_layer(x2d, seq, w_in, conv_w, w_out, g, b, ffn, alpha, tm=512, col_chunk=256):
    n, d = x2d.shape
    prev, cur, nxt = _halo_specs(tm, n, d)
    ffn_args, ffn_specs = _ffn_operands(ffn, d)
    return pl.pallas_call(
        functools.partial(_sc_kernel, tm=tm, tiles_per_seq=seq // tm, alpha=alpha, col_chunk=col_chunk),
        grid=(n // tm,),
        in_specs=[prev, cur, nxt, _resident(w_in.shape), _resident(conv_w.shape), _resident(w_out.shape),
                  _resident((1, d)), _resident((1, d))] + ffn_specs,
        out_specs=pl.BlockSpec((tm, d), lambda i: (i, 0)),
        out_shape=jax.ShapeDtypeStruct((n, d), F32),
        scratch_shapes=[pltpu.VMEM((tm, d), BF16)],
        compiler_params=_cparams(1),
        name="short_conv_mlp_layer",
    )(x2d, x2d, x2d, w_in, conv_w, w_out, g.reshape(1, d), b.reshape(1, d), *ffn_args)


def _ssd_in_kernel(xp_ref, x_ref, xn_ref, wz_ref, wxbc_ref, wdt_ref, cw_ref, cb_ref,
                   z_ref, xs_ref, bt_ref, cm_ref, dt_ref, y_scr, *, tm, tiles_per_seq, col_chunk):
    n = tm + 2 * HALO
    xcat = jnp.concatenate([xp_ref[...], x_ref[...], xn_ref[...]], axis=0).astype(BF16)
    xb = xcat[HALO:HALO + tm]
    valid = _halo_valid(tm, tiles_per_seq)
    n_conv = SSD_CONV_DIM // col_chunk
    n_gate = SSD_D_INNER // col_chunk
    for j in range(n_conv):
        if j % (n_conv // n_gate) == 0:
            glo = (j // (n_conv // n_gate)) * col_chunk
            z_ref[:, glo:glo + col_chunk] = _dot(xb, wz_ref[:, glo:glo + col_chunk]).astype(BF16)
        if j == n_conv - 1:
            dt_ref[...] = _dot(xb, wdt_ref[...])
        lo = j * col_chunk
        y = _dot(xcat, wxbc_ref[:, lo:lo + col_chunk])
        y = jnp.concatenate([jnp.where(valid[:HALO], y[:HALO], 0.0), y[HALO:HALO + tm],
                             jnp.where(valid[HALO + tm:], y[HALO + tm:], 0.0)], axis=0)
        half = SSD_CONV // 2
        convs = []
        for s in range(col_chunk // LANES):
            yb = y_scr.at[j % SSD_IN_YBUFS, s]
            yb[...] = y[:, s * LANES:(s + 1) * LANES]
            cw = cw_ref[:, lo + s * LANES:lo + (s + 1) * LANES]
            conv = cb_ref[:, lo + s * LANES:lo + (s + 1) * LANES]
            for k in range(SSD_CONV):
                conv = conv + cw[k:k + 1] * yb[HALO + k - half:HALO + k - half + tm, :]
            convs.append(conv)
        act = _silu(jnp.concatenate(convs, axis=1))
        if lo < SSD_D_INNER:
            xs_ref[:, lo:lo + col_chunk] = act.astype(BF16)
        elif lo < SSD_D_INNER + SSD_GN:
            o = lo - SSD_D_INNER
            bt_ref[o:o + col_chunk, :] = act.T.astype(BF16)
        else:
            o = lo - SSD_D_INNER - SSD_GN
            cm_ref[:, o:o + col_chunk] = act.astype(BF16)


def _ssd_in(x2d, seq, wz, wxbc, wdt, conv_w, conv_b, tm=1024, col_chunk=256):
    n, d = x2d.shape
    prev, cur, nxt = _halo_specs(tm, n, d)
    row = lambda w: pl.BlockSpec((tm, w), lambda i: (i, 0))
    return pl.pallas_call(
        functools.partial(_ssd_in_kernel, tm=tm, tiles_per_seq=seq // tm, col_chunk=col_chunk),
        grid=(n // tm,),
        in_specs=[prev, cur, nxt, _resident(wz.shape), _resident(wxbc.shape), _resident(wdt.shape),
                  _resident(conv_w.shape), _resident((1, SSD_CONV_DIM))],
        out_specs=[row(SSD_D_INNER), row(SSD_D_INNER), pl.BlockSpec((SSD_GN, tm), lambda i: (0, i)),
                   row(SSD_GN), row(SSD_DT_PAD)],
        out_shape=[jax.ShapeDtypeStruct((n, SSD_D_INNER), BF16), jax.ShapeDtypeStruct((n, SSD_D_INNER), BF16),
                   jax.ShapeDtypeStruct((SSD_GN, n), BF16), jax.ShapeDtypeStruct((n, SSD_GN), BF16),
                   jax.ShapeDtypeStruct((n, SSD_DT_PAD), F32)],
        scratch_shapes=[pltpu.VMEM((SSD_IN_YBUFS, col_chunk // LANES, tm + 2 * HALO, LANES), F32)],
        compiler_params=_cparams(1),
        name="ssd_in_proj_conv",
    )(x2d, x2d, x2d, wz, wxbc, wdt, conv_w, conv_b.reshape(1, SSD_CONV_DIM))


def _split3(v):
    nlive = 2 * SSD_HEADS
    live = lax.broadcasted_iota(jnp.int32, (1, SSD_DT_PAD), 1) < nlive
    hi = v.astype(BF16)
    rem = v - hi.astype(F32)
    rem2 = rem - rem.astype(BF16).astype(F32)
    first = jnp.where(live, v, pltpu.roll(rem, nlive, 1)).astype(BF16)
    second = jnp.where(live, rem2, 0.0).astype(BF16)
    return jnp.concatenate([first, second], axis=1)


def _ssd_prep_kernel(dt_ref, bias_ref, acoef_ref, p1_ref, p2_ref, p3_ref, r1_ref, r2_ref, r4_ref, *, tl):
    rows = lax.broadcasted_iota(jnp.int32, (SSD_CHUNK, 1), 0)
    lane = lax.broadcasted_iota(jnp.int32, (1, SSD_DT_PAD), 1)
    is_fwd = (lane // SSD_HPG) % 2 == 0
    for c in range(tl // SSD_CHUNK):
        sl = slice(c * SSD_CHUNK, (c + 1) * SSD_CHUNK)
        raw = dt_ref[0, sl, :] + bias_ref[...]
        dt = jnp.maximum(raw, 0.0) + jnp.log1p(jnp.exp(-jnp.abs(raw)))
        a = dt * acoef_ref[...]
        cs = a
        sh = 1
        while sh < SSD_CHUNK:
            cs = cs + jnp.where(rows >= sh, pltpu.roll(cs, sh, 0), 0.0)
            sh *= 2
        tot = cs[SSD_CHUNK - 1:SSD_CHUNK, :]
        ec = cs - a
        p1 = jnp.where(is_fwd, cs, ec)
        p2 = jnp.where(is_fwd, jnp.exp(cs), jnp.exp(tot - ec))
        p3 = jnp.where(is_fwd, jnp.exp(tot - cs), jnp.exp(ec)) * dt
        etot = jnp.broadcast_to(jnp.exp(tot), (SSD_CHUNK, SSD_DT_PAD))
        p1_ref[0, sl, :] = _split3(p1 * LOG2_E)
        p2_ref[0, sl, :] = _split3(p2)
        p3_ref[0, sl, :] = _split3(p3)
        log_dt = jnp.log(dt)
        dt_pair = dt + jnp.where(is_fwd, pltpu.roll(dt, SSD_DT_PAD - SSD_HPG, 1), pltpu.roll(dt, SSD_HPG, 1))
        r1_ref[0, :, sl] = (jnp.where(is_fwd, cs - log_dt, ec + log_dt) * LOG2_E).T
        r2_ref[0, :, sl] = (jnp.log(dt_pair) * LOG2_E).T
        r4_ref[0, :, sl] = etot.T


def _ssd_prep(dt_raw, bias, acoef, tl=1024):
    bsz, seq, w = dt_raw.shape
    col_in = pl.BlockSpec((1, tl, w), lambda b, i: (b, i, 0))
    col = pl.BlockSpec((1, tl, 2 * w), lambda b, i: (b, i, 0))
    rowm = pl.BlockSpec((1, w, tl), lambda b, i: (b, 0, i))
    col_shape = jax.ShapeDtypeStruct((bsz, seq, 2 * w), BF16)
    row_shape = jax.ShapeDtypeStruct((bsz, w, seq), F32)
    return pl.pallas_call(
        functools.partial(_ssd_prep_kernel, tl=tl),
        grid=(bsz, seq // tl),
        in_specs=[col_in, pl.BlockSpec((1, w), lambda b, i: (0, 0)), pl.BlockSpec((1, w), lambda b, i: (0, 0))],
        out_specs=[col, col, col, rowm, rowm, rowm],
        out_shape=[col_shape, col_shape, col_shape, row_shape, row_shape, row_shape],
        compiler_params=_cparams(2),
        name="ssd_decay_tables",
    )(dt_raw, bias, acoef)


def _head_lanes(vals, lane):
    out = vals[SSD_HPG - 1]
    for k in range(SSD_HPG - 2, -1, -1):
        out = jnp.where(lane < (k + 1) * SSD_HEAD_DIM, vals[k], out)
    return out


def _ssd_core_kernel(xs_ref, bt_ref, cm_ref, z_ref, p1_ref, p2_ref, p3_ref, r1_ref, r2_ref, r4_ref,
                     dsk_ref, nw_ref, ef_ref, el_ref, o_ref, st_scr, xw_scr, *seg_scr, nchunks):
    q = SSD_CHUNK
    hpg = SSD_HPG
    lane = lax.broadcasted_iota(jnp.int32, (1, SSD_GW), 1)

    def chunk_decay(c, direction):
        blk = r4_ref[0, :, c * q:(c + 1) * q]
        rows = [jnp.concatenate([blk[direction * hpg + k:direction * hpg + k + 1, :]] * (SSD_GW // q), axis=1)
                for k in range(hpg)]
        return _head_lanes(rows, lane)

    def chunk_rows(c):
        return pl.ds(pl.multiple_of(c * q, q), q)

    xs_all = xs_ref[0].astype(F32)
    w3 = _dot(p3_ref[0], ef_ref[0])
    xw_scr[...] = (jnp.concatenate([xs_all, xs_all], axis=1) * w3).astype(BF16)

    nseg = len(seg_scr) // 2
    f2_seg, colv_seg = seg_scr[:nseg], seg_scr[nseg:]
    seg_chunks = nchunks // nseg

    def expand_output_tables(seg, j):
        rows = chunk_rows(seg * seg_chunks + j)
        local = chunk_rows(j)
        f2_seg[seg][local, :] = _dot(p2_ref[0, rows, :], ef_ref[0])
        colv_seg[seg][local, :] = _dot(p1_ref[0, rows, :], el_ref[0])

    for j in range(seg_chunks):
        expand_output_tables(0, j)

    def state_body(c, carry):
        rs = chunk_rows(c)
        st = _dot(bt_ref[:, rs], xw_scr[rs, :])
        st_scr[c, 0] = st[:, :SSD_GW]
        st_scr[c, 1] = st[:, SSD_GW:]
        return carry

    lax.fori_loop(0, nchunks, state_body, 0, unroll=4)

    s_f = jnp.zeros((SSD_STATE, SSD_GW), F32)
    for c in range(nchunks):
        add = st_scr[c, 0]
        st_scr[c, 0] = s_f
        s_f = chunk_decay(c, 0) * s_f + add
    s_b = jnp.zeros((SSD_STATE, SSD_GW), F32)
    for c in range(nchunks - 1, -1, -1):
        add = st_scr[c, 1]
        st_scr[c, 1] = s_b
        s_b = chunk_decay(c, 1) * s_b + add

    ti = lax.broadcasted_iota(jnp.int32, (q, q), 0)
    si = lax.broadcasted_iota(jnp.int32, (q, q), 1)
    lower_strict = si < ti
    upper_strict = si > ti

    def out_chunk(seg, j):
        c = seg * seg_chunks + j
        rs = chunk_rows(c)
        local = chunk_rows(j)
        colv_scr, f2_scr = colv_seg[seg], f2_seg[seg]
        xs_b = xs_ref[0, rs, :]
        cm = cm_ref[0, rs, :]
        g = _dot(cm, bt_ref[:, rs])
        r1 = r1_ref[0, :, rs]
        r2 = r2_ref[0, :, rs]
        ms = []
        for k in range(hpg):
            kb = hpg + k
            seg_exp = jnp.where(lower_strict, colv_scr[local, k * q:(k + 1) * q] - r1[k:k + 1, :],
                                jnp.where(upper_strict, r1[kb:kb + 1, :] - colv_scr[local, kb * q:(kb + 1) * q],
                                          r2[k:k + 1, :]))
            ms.append((g * jnp.exp2(seg_exp)).astype(BF16))
        zero = jnp.zeros_like(xs_b)
        xbd = jnp.concatenate(
            [jnp.where(jnp.logical_and(lane >= k * SSD_HEAD_DIM, lane < (k + 1) * SSD_HEAD_DIM), xs_b, zero)
             for k in range(hpg)], axis=0)
        y = _dot(jnp.concatenate(ms, axis=1), xbd)
        s_in = jnp.concatenate([st_scr[c, 0], st_scr[c, 1]], axis=1).astype(BF16)
        y_off = _dot(cm, s_in)
        xs = xs_b.astype(F32)
        f2 = f2_scr[local, :]
        y = y + f2[:, :SSD_GW] * y_off[:, :SSD_GW] + f2[:, SSD_GW:] * y_off[:, SSD_GW:] + dsk_ref[0] * xs
        z = z_ref[0, rs, :].astype(F32)
        y = y * _silu(z)
        y = y * lax.rsqrt(jnp.mean(y * y, axis=-1, keepdims=True) + EPS)
        o_ref[0, rs, :] = (y * nw_ref[0]).astype(BF16)

    for seg in range(nseg):
        def seg_body(j, carry, seg=seg):
            out_chunk(seg, j)
            if seg + 1 < nseg:
                expand_output_tables(seg + 1, j)
            return carry

        lax.fori_loop(0, seg_chunks, seg_body, 0, unroll=True)


def _expansion_matrices():
    nhd = 2 * SSD_HPG
    nlive = 2 * SSD_HEADS
    row = jnp.arange(2 * SSD_DT_PAD)[None, :, None]
    grp = jnp.arange(SSD_GROUPS)[:, None, None]
    col_of_row = row % nlive - grp * nhd
    live = row < 3 * nlive
    head_lane = jnp.arange(2 * SSD_GW)[None, None, :] // SSD_HEAD_DIM
    block_lane = jnp.arange(nhd * SSD_CHUNK)[None, None, :] // SSD_CHUNK
    ef = jnp.where(live & (col_of_row == head_lane), 1.0, 0.0).astype(BF16)
    el = jnp.where(live & (col_of_row == block_lane), 1.0, 0.0).astype(BF16)
    return ef, el


def _ssd_core(xs, bt, cm, z, p1, p2, p3, r1, r2, r4, dsk, nw):
    bsz, seq, _ = xs.shape
    nchunks = seq // SSD_CHUNK
    ef, el = _expansion_matrices()
    wide = pl.BlockSpec((1, seq, SSD_GW), lambda b, g: (b, 0, g))
    narrow = pl.BlockSpec((1, seq, SSD_STATE), lambda b, g: (b, 0, g))
    state_major = pl.BlockSpec((SSD_STATE, seq), lambda b, g: (g, b))
    colt = pl.BlockSpec((1, seq, 2 * SSD_DT_PAD), lambda b, g: (b, 0, 0))
    rowt = pl.BlockSpec((1, 2 * SSD_HPG, seq), lambda b, g: (b, g, 0))
    pergroup = pl.BlockSpec((1, 1, SSD_GW), lambda b, g: (g, 0, 0))
    return pl.pallas_call(
        functools.partial(_ssd_core_kernel, nchunks=nchunks),
        grid=(bsz, SSD_GROUPS),
        in_specs=[wide, state_major, narrow, wide, colt, colt, colt, rowt, rowt, rowt, pergroup, pergroup,
                  pl.BlockSpec((1,) + ef.shape[1:], lambda b, g: (g, 0, 0)),
                  pl.BlockSpec((1,) + el.shape[1:], lambda b, g: (g, 0, 0))],
        out_specs=wide,
        out_shape=jax.ShapeDtypeStruct((bsz, seq, SSD_D_INNER), BF16),
        scratch_shapes=([pltpu.VMEM((nchunks, 2, SSD_STATE, SSD_GW), F32),
                         pltpu.VMEM((seq, 2 * SSD_GW), BF16)]
                        + [pltpu.VMEM((seq // SSD_SCAN_SEGMENTS, 2 * SSD_GW), F32)] * SSD_SCAN_SEGMENTS
                        + [pltpu.VMEM((seq // SSD_SCAN_SEGMENTS, 2 * SSD_HPG * SSD_CHUNK), F32)] * SSD_SCAN_SEGMENTS),
        compiler_params=_cparams(2),
        name="ssd_scan",
    )(xs, bt, cm, z, p1, p2, p3, r1, r2, r4, dsk, nw, ef, el)


def _ssd_layer(x2d, bsz, seq, w_in, conv_w, conv_b, dt_bias, a_log, d_skip, norm_w, w_out, g, b, ffn, alpha):
    n = x2d.shape[0]
    o_xbc = SSD_D_INNER
    o_dt = SSD_D_INNER + SSD_CONV_DIM
    perm = jnp.array([dr * SSD_HEADS + gi * SSD_HPG + k
                      for gi in range(SSD_GROUPS) for dr in range(2) for k in range(SSD_HPG)], jnp.int32)
    pad = SSD_DT_PAD - 2 * SSD_HEADS
    wz = w_in[:, :o_xbc].astype(BF16)
    wxbc = w_in[:, o_xbc:o_dt].astype(BF16)
    wdt = jnp.pad(w_in[:, o_dt:][:, perm], ((0, 0), (0, pad))).astype(BF16)
    bias = jnp.pad(dt_bias.reshape(-1)[perm], (0, pad)).reshape(1, SSD_DT_PAD)
    acoef = jnp.pad(a_log.reshape(-1)[perm], (0, pad)).reshape(1, SSD_DT_PAD)
    acoef = jnp.where(jnp.arange(SSD_DT_PAD) < 2 * SSD_HEADS, -jnp.exp(acoef), 0.0).reshape(1, SSD_DT_PAD)

    z, xs, bt, cm, dt_raw = _ssd_in(x2d, seq, wz, wxbc, wdt, conv_w, conv_b)
    tables = _ssd_prep(dt_raw.reshape(bsz, seq, SSD_DT_PAD), bias, acoef)
    dsk = jnp.repeat(d_skip, SSD_HEAD_DIM).reshape(SSD_GROUPS, 1, SSD_GW)
    nw = norm_w.reshape(SSD_GROUPS, 1, SSD_GW)
    y = _ssd_core(xs.reshape(bsz, seq, -1), bt, cm.reshape(bsz, seq, -1),
                  z.reshape(bsz, seq, -1), *tables, dsk, nw)
    return _proj_mlp(y.reshape(n, SSD_D_INNER), x2d, w_out.astype(BF16), g, b, ffn, alpha)


def _qkv_kernel(x_ref, w_ref, q_ref, k_ref, v_ref, *, col_chunk):
    d = D_MODEL
    xb = x_ref[...].astype(BF16)
    for j in range(d // col_chunk):
        lo = j * col_chunk
        q_ref[:, lo:lo + col_chunk] = (_dot(xb, w_ref[:, lo:lo + col_chunk]) * NA_Q_SCALE).astype(BF16)
        k_ref[:, lo:lo + col_chunk] = _dot(xb, w_ref[:, d + lo:d + lo + col_chunk]).astype(BF16)
        v_ref[:, lo:lo + col_chunk] = _dot(xb, w_ref[:, 2 * d + lo:2 * d + lo + col_chunk]).astype(BF16)


def _qkv(x2d, w, tm=512, col_chunk=512):
    n, d = x2d.shape
    spec = pl.BlockSpec((tm, d), lambda i: (i, 0))
    shape = jax.ShapeDtypeStruct((n, d), BF16)
    return pl.pallas_call(
        functools.partial(_qkv_kernel, col_chunk=col_chunk),
        grid=(n // tm,),
        in_specs=[spec, _resident(w.shape)],
        out_specs=[spec, spec, spec],
        out_shape=[shape, shape, shape],
        compiler_params=_cparams(1),
        name="na_qkv",
    )(x2d, w)


def _na_core_kernel(q_ref, k_ref, v_ref, bias_ref, o_ref, s_scr, p_scr, *, rows):
    r = pl.program_id(1)
    start = pl.multiple_of(jnp.clip(r - NA_WIN_ROWS // 2, 0, rows - NA_WIN_ROWS) * GRID_W, GRID_W)
    keys = pl.ds(start, NA_KEYS)
    pair = 2 * NA_HEAD_DIM
    n_pairs = NA_HEADS // 2
    lo_half = lax.broadcasted_iota(jnp.int32, (1, pair), 1) < NA_HEAD_DIM
    q = q_ref[0, 0]
    for hp in range(n_pairs):
        sl = slice(hp * pair, (hp + 1) * pair)
        qp = q[:, sl]
        zero = jnp.zeros_like(qp)
        qm = jnp.concatenate([jnp.where(lo_half, qp, zero), jnp.where(lo_half, zero, qp)], axis=0)
        s = lax.dot_general(qm, k_ref[0, keys, sl], (((1,), (1,)), ((), ())), preferred_element_type=F32)
        s_scr[hp] = s + bias_ref[0, 2 * hp:2 * hp + 2].reshape(2 * GRID_W, NA_KEYS)
    denoms = []
    for hp in range(n_pairs):
        s = s_scr[hp]
        p = jnp.exp2(s - jnp.max(s, axis=-1, keepdims=True))
        denoms.append(jnp.sum(p, axis=-1, keepdims=True))
        p_scr[hp] = p.astype(BF16)
    for hp in range(n_pairs):
        sl = slice(hp * pair, (hp + 1) * pair)
        o = _dot(p_scr[hp], v_ref[0, keys, sl]) / denoms[hp]
        o_ref[0, 0, :, sl] = jnp.where(lo_half, o[:GRID_W], o[GRID_W:]).astype(BF16)


def _na_core(q, k, v, bias, bsz, seq):
    rows = seq // GRID_W
    d = D_MODEL
    half = NA_WIN_ROWS // 2

    def bias_idx(b, r):
        return (jnp.where(r < half, r, jnp.where(r <= rows - half, half, r - (rows - NA_WIN_ROWS))), 0, 0, 0)

    qspec = pl.BlockSpec((1, 1, GRID_W, d), lambda b, r: (b, r, 0, 0))
    kvspec = pl.BlockSpec((1, seq, d), lambda b, r: (b, 0, 0))
    return pl.pallas_call(
        functools.partial(_na_core_kernel, rows=rows),
        grid=(bsz, rows),
        in_specs=[qspec, kvspec, kvspec, pl.BlockSpec((1, NA_HEADS, GRID_W, NA_KEYS), bias_idx)],
        out_specs=qspec,
        out_shape=jax.ShapeDtypeStruct((bsz, rows, GRID_W, d), BF16),
        scratch_shapes=[pltpu.VMEM((NA_HEADS // 2, 2 * GRID_W, NA_KEYS), F32),
                        pltpu.VMEM((NA_HEADS // 2, 2 * GRID_W, NA_KEYS), BF16)],
        compiler_params=_cparams(2),
        name="na_attention",
    )(q.reshape(bsz, rows, GRID_W, d), k.reshape(bsz, seq, d), v.reshape(bsz, seq, d), bias)


def _na_bias_table(rpb):
    w = GRID_W
    n_dr = 2 * NA_WIN_ROWS - 1
    ext = jnp.concatenate([jnp.repeat(rpb[:, :, :1], w - NA_WIN_COLS, axis=2), rpb.astype(F32),
                           jnp.repeat(rpb[:, :, -1:], 2 * w - (w - NA_WIN_COLS) - (2 * NA_WIN_COLS - 1), axis=2)],
                          axis=2)
    flat = jnp.broadcast_to(ext[:, :, None, :], (NA_HEADS, n_dr, w, 2 * w)).reshape(NA_HEADS, n_dr, 2 * w * w)
    toep = flat[:, :, w - 1:w - 1 + w * (2 * w - 1)].reshape(NA_HEADS, n_dr, w, 2 * w - 1)[..., :w]
    qc = jnp.arange(w)
    kc = jnp.arange(w)
    col_start = jnp.clip(qc - NA_WIN_COLS // 2, 0, w - NA_WIN_COLS)
    in_win = (kc[None, :] >= col_start[:, None]) & (kc[None, :] < col_start[:, None] + NA_WIN_COLS)
    toep = jnp.where(in_win[None, None], toep, -jnp.inf)
    per_offset = [toep[:, NA_WIN_ROWS - 1 - o:2 * NA_WIN_ROWS - 1 - o].transpose(0, 2, 1, 3)
                  for o in range(NA_WIN_ROWS)]
    return jnp.stack(per_offset, axis=0).reshape(NA_WIN_ROWS, NA_HEADS, w, NA_KEYS) * LOG2_E


def _na_layer(x2d, bsz, seq, w_qkv, rpb, w_out, g, b, ffn, alpha):
    n, d = x2d.shape
    q, k, v = _qkv(x2d, w_qkv.astype(BF16))
    o = _na_core(q, k, v, _na_bias_table(rpb), bsz, seq)
    return _proj_mlp(o.reshape(n, d), x2d, w_out.astype(BF16), g, b, ffn, alpha)


def kernel(x, ln_mix_g, ln_mix_b, ln_ffn_g, ln_ffn_b, ffn_w_in, ffn_w_out, ssd_w_in, ssd_conv_w, ssd_conv_b,
           ssd_dt_bias, ssd_a_log, ssd_d, ssd_norm_w, ssd_w_out, sc_w_in, sc_conv_w, sc_w_out,
           na_w_qkv, na_rpb, na_w_out):
    bsz, seq, d = x.shape
    alpha = (2.0 * DEPTH) ** 0.25
    h = x.reshape(bsz * seq, d)
    for i in range(DEPTH):
        kind = i % N_MIXERS
        j = i // N_MIXERS
        ffn = (ffn_w_in[i].astype(BF16), ffn_w_out[i].astype(BF16), ln_ffn_g[i], ln_ffn_b[i])
        if kind == 0:
            h = _ssd_layer(h, bsz, seq, ssd_w_in[j], ssd_conv_w[j], ssd_conv_b[j], ssd_dt_bias[j], ssd_a_log[j],
                           ssd_d[j], ssd_norm_w[j], ssd_w_out[j], ln_mix_g[i], ln_mix_b[i], ffn, alpha)
        elif kind == 1:
            h = _sc_layer(h, seq, sc_w_in[j].astype(BF16), sc_conv_w[j], sc_w_out[j].astype(BF16),
                          ln_mix_g[i], ln_mix_b[i], ffn, alpha)
        else:
            h = _na_layer(h, bsz, seq, na_w_qkv[j], na_rpb[j], na_w_out[j], ln_mix_g[i], ln_mix_b[i], ffn, alpha)
    return h.reshape(bsz, seq, d)
```

```python
import functools

import jax
import jax.numpy as jnp
from jax import lax
from jax.experimental import pallas as pl
from jax.experimental.pallas import tpu as pltpu

F32 = jnp.float32
BF16 = jnp.bfloat16

D_MODEL = 1024
DEPTH = 4
N_MIXERS = 3
FFN_DIM = 4 * D_MODEL
EPS = 1e-5

SSD_D_INNER = 2 * D_MODEL
SSD_HEAD_DIM = 64
SSD_HEADS = SSD_D_INNER // SSD_HEAD_DIM
SSD_GROUPS = 8
SSD_HPG = SSD_HEADS // SSD_GROUPS
SSD_GW = SSD_HPG * SSD_HEAD_DIM
SSD_STATE = 128
SSD_CONV = 5
SSD_CHUNK = 128
SSD_GN = SSD_GROUPS * SSD_STATE
SSD_CONV_DIM = SSD_D_INNER + 2 * SSD_GN
SSD_DT_PAD = 128
SSD_SCAN_SEGMENTS = 1

SC_CONV = 3

GRID_W = 64
NA_HEADS = 16
NA_HEAD_DIM = D_MODEL // NA_HEADS
NA_WIN_ROWS = 8
NA_WIN_COLS = 16
NA_KEYS = NA_WIN_ROWS * GRID_W
NA_ROWS_PER_STEP = 2
NA_SOFTMAX_ROWS = 32
LOG2_E = 1.4426950408889634
NA_Q_SCALE = NA_HEAD_DIM ** -0.5 * LOG2_E

HALO = 8
LANES = 128
SSD_IN_YBUFS = 2
VMEM_LIMIT = 56 * 1024 * 1024


def _cparams(n_axes):
    return pltpu.CompilerParams(dimension_semantics=("arbitrary",) * n_axes,
                                vmem_limit_bytes=VMEM_LIMIT)


def _resident(shape):
    nd = len(shape)
    return pl.BlockSpec(shape, lambda *_: (0,) * nd, pipeline_mode=pl.Buffered(1))


def _dot(a, b):
    return jnp.dot(a, b, preferred_element_type=F32)


def _silu(x):
    h = 0.5 * x
    return h + h * jnp.tanh(h)


def _layer_norm(y, g, b):
    mu = jnp.mean(y, axis=-1, keepdims=True)
    yc = y - mu
    var = jnp.mean(yc * yc, axis=-1, keepdims=True)
    return yc * lax.rsqrt(var + EPS) * g + b


def _halo_specs(tm, n_rows, width):
    blocks_per_tile = tm // HALO
    last_block = n_rows // HALO - 1
    prev = pl.BlockSpec((HALO, width), lambda i: (jnp.maximum(i * blocks_per_tile - 1, 0), 0))
    cur = pl.BlockSpec((tm, width), lambda i: (i, 0))
    nxt = pl.BlockSpec((HALO, width), lambda i: (jnp.minimum((i + 1) * blocks_per_tile, last_block), 0))
    return prev, cur, nxt


def _halo_valid(tm, tiles_per_seq):
    i = pl.program_id(0)
    first = (i % tiles_per_seq) == 0
    last = (i % tiles_per_seq) == tiles_per_seq - 1
    rows = lax.broadcasted_iota(jnp.int32, (tm + 2 * HALO, 1), 0)
    ok_top = jnp.logical_or(rows >= HALO, jnp.logical_not(first))
    ok_bot = jnp.logical_or(rows < tm + HALO, jnp.logical_not(last))
    return jnp.logical_and(ok_top, ok_bot)


FF_CHUNK = 1024


def _mlp_ln_tail(x1, w1_ref, w2_ref, g_ref, b_ref, alpha):
    xb = x1.astype(BF16)
    acc = alpha * x1
    for c in range(FFN_DIM // FF_CHUNK):
        sl = slice(c * FF_CHUNK, (c + 1) * FF_CHUNK)
        h = jnp.maximum(_dot(xb, w1_ref[:, sl]), 0.0)
        acc = acc + _dot((h * h).astype(BF16), w2_ref[sl, :])
    return _layer_norm(acc, g_ref[...], b_ref[...])


def _ffn_operands(ffn, d):
    w1, w2, g, b = ffn
    return ([w1, w2, g.reshape(1, d), b.reshape(1, d)],
            [_resident(w1.shape), _resident(w2.shape), _resident((1, d)), _resident((1, d))])


def _proj_mlp_kernel(y_ref, x_ref, w_ref, g_ref, b_ref, w1_ref, w2_ref, g2_ref, b2_ref, o_ref, *, alpha):
    x1 = _layer_norm(alpha * x_ref[...] + _dot(y_ref[...], w_ref[...]), g_ref[...], b_ref[...])
    o_ref[...] = _mlp_ln_tail(x1, w1_ref, w2_ref, g2_ref, b2_ref, alpha)


def _proj_mlp(y2d, x2d, w, g, b, ffn, alpha, tm=512):
    n, d = x2d.shape
    k = y2d.shape[1]
    ffn_args, ffn_specs = _ffn_operands(ffn, d)
    return pl.pallas_call(
        functools.partial(_proj_mlp_kernel, alpha=alpha),
        grid=(n // tm,),
        in_specs=[pl.BlockSpec((tm, k), lambda i: (i, 0)),
                  pl.BlockSpec((tm, d), lambda i: (i, 0)),
                  _resident(w.shape), _resident((1, d)), _resident((1, d))] + ffn_specs,
        out_specs=pl.BlockSpec((tm, d), lambda i: (i, 0)),
        out_shape=jax.ShapeDtypeStruct((n, d), F32),
        compiler_params=_cparams(1),
        name="proj_ln_mlp_ln",
    )(y2d, x2d, w, g.reshape(1, d), b.reshape(1, d), *ffn_args)


def _sc_kernel(xp_ref, x_ref, xn_ref, win_ref, cw_ref, wout_ref, g_ref, b_ref, w1_ref, w2_ref, g2_ref, b2_ref,
               o_ref, v_scr, *, tm, tiles_per_seq, alpha, col_chunk):
    d = D_MODEL
    n = tm + 2 * HALO
    x = x_ref[...]
    xcat = jnp.concatenate([xp_ref[...], x, xn_ref[...]], axis=0).astype(BF16)
    xb = xcat[HALO:HALO + tm]
    valid = _halo_valid(tm, tiles_per_seq)
    for j in range(d // col_chunk):
        lo = j * col_chunk
        b_gate = _dot(xb, win_ref[:, lo:lo + col_chunk])
        c_gate = _dot(xcat, win_ref[:, d + lo:d + lo + col_chunk])
        h = _dot(xcat, win_ref[:, 2 * d + lo:2 * d + lo + col_chunk])
        u = jnp.where(valid, c_gate * h, 0.0)
        cw = cw_ref[:, lo:lo + col_chunk]
        conv = (cw[0:1] * pltpu.roll(u, 1, 0) + cw[1:2] * u + cw[2:3] * pltpu.roll(u, n - 1, 0))
        v_scr[:, lo:lo + col_chunk] = (b_gate * conv[HALO:HALO + tm]).astype(BF16)
    x1 = _layer_norm(alpha * x + _dot(v_scr[...], wout_ref[...]), g_ref[...], b_ref[...])
    o_ref[...] = _mlp_ln_tail(x1, w1_ref, w2_ref, g2_ref, b2_ref, alpha)


def _sc_layer(x2d, seq, w_in, conv_w, w_out, g, b, ffn, alpha, tm=512, col_chunk=256):
    n, d = x2d.shape
    prev, cur, nxt = _halo_specs(tm, n, d)
    ffn_args, ffn_specs = _ffn_operands(ffn, d)
    return pl.pallas_call(
        functools.partial(_sc_kernel, tm=tm, tiles_per_seq=seq // tm, alpha=alpha, col_chunk=col_chunk),
        grid=(n // tm,),
        in_specs=[prev, cur, nxt, _resident(w_in.shape), _resident(conv_w.shape), _resident(w_out.shape),
                  _resident((1, d)), _resident((1, d))] + ffn_specs,
        out_specs=pl.BlockSpec((tm, d), lambda i: (i, 0)),
        out_shape=jax.ShapeDtypeStruct((n, d), F32),
        scratch_shapes=[pltpu.VMEM((tm, d), BF16)],
        compiler_params=_cparams(1),
        name="short_conv_mlp_layer",
    )(x2d, x2d, x2d, w_in, conv_w, w_out, g.reshape(1, d), b.reshape(1, d), *ffn_args)


def _ssd_in_kernel(xp_ref, x_ref, xn_ref, wz_ref, wxbc_ref, wdt_ref, cw_ref, cb_ref,
                   z_ref, xs_ref, bt_ref, cm_ref, dt_ref, y_scr, *, tm, tiles_per_seq, col_chunk):
    n = tm + 2 * HALO
    xcat = jnp.concatenate([xp_ref[...], x_ref[...], xn_ref[...]], axis=0).astype(BF16)
    xb = xcat[HALO:HALO + tm]
    valid = _halo_valid(tm, tiles_per_seq)
    n_conv = SSD_CONV_DIM // col_chunk
    n_gate = SSD_D_INNER // col_chunk
    for j in range(n_conv):
        if j % (n_conv // n_gate) == 0:
            glo = (j // (n_conv // n_gate)) * col_chunk
            z_ref[:, glo:glo + col_chunk] = _dot(xb, wz_ref[:, glo:glo + col_chunk]).astype(BF16)
        if j == n_conv - 1:
            dt_ref[...] = _dot(xb, wdt_ref[...])
        lo = j * col_chunk
        y = _dot(xcat, wxbc_ref[:, lo:lo + col_chunk])
        y = jnp.concatenate([jnp.where(valid[:HALO], y[:HALO], 0.0), y[HALO:HALO + tm],
                             jnp.where(valid[HALO + tm:], y[HALO + tm:], 0.0)], axis=0)
        half = SSD_CONV // 2
        convs = []
        for s in range(col_chunk // LANES):
            yb = y_scr.at[j % SSD_IN_YBUFS, s]
            yb[...] = y[:, s * LANES:(s + 1) * LANES]
            cw = cw_ref[:, lo + s * LANES:lo + (s + 1) * LANES]
            conv = cb_ref[:, lo + s * LANES:lo + (s + 1) * LANES]
            for k in range(SSD_CONV):
                conv = conv + cw[k:k + 1] * yb[HALO + k - half:HALO + k - half + tm, :]
            convs.append(conv)
        act = _silu(jnp.concatenate(convs, axis=1))
        if lo < SSD_D_INNER:
            xs_ref[:, lo:lo + col_chunk] = act.astype(BF16)
        elif lo < SSD_D_INNER + SSD_GN:
            o = lo - SSD_D_INNER
            bt_ref[o:o + col_chunk, :] = act.T.astype(BF16)
        else:
            o = lo - SSD_D_INNER - SSD_GN
            cm_ref[:, o:o + col_chunk] = act.astype(BF16)


def _ssd_in(x2d, seq, wz, wxbc, wdt, conv_w, conv_b, tm=1024, col_chunk=256):
    n, d = x2d.shape
    prev, cur, nxt = _halo_specs(tm, n, d)
    row = lambda w: pl.BlockSpec((tm, w), lambda i: (i, 0))
    return pl.pallas_call(
        functools.partial(_ssd_in_kernel, tm=tm, tiles_per_seq=seq // tm, col_chunk=col_chunk),
        grid=(n // tm,),
        in_specs=[prev, cur, nxt, _resident(wz.shape), _resident(wxbc.shape), _resident(wdt.shape),
                  _resident(conv_w.shape), _resident((1, SSD_CONV_DIM))],
        out_specs=[row(SSD_D_INNER), row(SSD_D_INNER), pl.BlockSpec((SSD_GN, tm), lambda i: (0, i)),
                   row(SSD_GN), row(SSD_DT_PAD)],
        out_shape=[jax.ShapeDtypeStruct((n, SSD_D_INNER), BF16), jax.ShapeDtypeStruct((n, SSD_D_INNER), BF16),
                   jax.ShapeDtypeStruct((SSD_GN, n), BF16), jax.ShapeDtypeStruct((n, SSD_GN), BF16),
                   jax.ShapeDtypeStruct((n, SSD_DT_PAD), F32)],
        scratch_shapes=[pltpu.VMEM((SSD_IN_YBUFS, col_chunk // LANES, tm + 2 * HALO, LANES), F32)],
        compiler_params=_cparams(1),
        name="ssd_in_proj_conv",
    )(x2d, x2d, x2d, wz, wxbc, wdt, conv_w, conv_b.reshape(1, SSD_CONV_DIM))


def _split3(v):
    nlive = 2 * SSD_HEADS
    live = lax.broadcasted_iota(jnp.int32, (1, SSD_DT_PAD), 1) < nlive
    hi = v.astype(BF16)
    rem = v - hi.astype(F32)
    rem2 = rem - rem.astype(BF16).astype(F32)
    first = jnp.where(live, v, pltpu.roll(rem, nlive, 1)).astype(BF16)
    second = jnp.where(live, rem2, 0.0).astype(BF16)
    return jnp.concatenate([first, second], axis=1)


def _ssd_prep_kernel(dt_ref, bias_ref, acoef_ref, p1_ref, p2_ref, p3_ref, r1_ref, r2_ref, r4_ref, *, tl):
    rows = lax.broadcasted_iota(jnp.int32, (SSD_CHUNK, 1), 0)
    lane = lax.broadcasted_iota(jnp.int32, (1, SSD_DT_PAD), 1)
    is_fwd = (lane // SSD_HPG) % 2 == 0
    for c in range(tl // SSD_CHUNK):
        sl = slice(c * SSD_CHUNK, (c + 1) * SSD_CHUNK)
        raw = dt_ref[0, sl, :] + bias_ref[...]
        dt = jnp.maximum(raw, 0.0) + jnp.log1p(jnp.exp(-jnp.abs(raw)))
        a = dt * acoef_ref[...]
        cs = a
        sh = 1
        while sh < SSD_CHUNK:
            cs = cs + jnp.where(rows >= sh, pltpu.roll(cs, sh, 0), 0.0)
            sh *= 2
        tot = cs[SSD_CHUNK - 1:SSD_CHUNK, :]
        ec = cs - a
        p1 = jnp.where(is_fwd, cs, ec)
        p2 = jnp.where(is_fwd, jnp.exp(cs), jnp.exp(tot - ec))
        p3 = jnp.where(is_fwd, jnp.exp(tot - cs), jnp.exp(ec)) * dt
        etot = jnp.broadcast_to(jnp.exp(tot), (SSD_CHUNK, SSD_DT_PAD))
        p1_ref[0, sl, :] = _split3(p1 * LOG2_E)
        p2_ref[0, sl, :] = _split3(p2)
        p3_ref[0, sl, :] = _split3(p3)
        log_dt = jnp.log(dt)
        dt_pair = dt + jnp.where(is_fwd, pltpu.roll(dt, SSD_DT_PAD - SSD_HPG, 1), pltpu.roll(dt, SSD_HPG, 1))
        r1_ref[0, :, sl] = (jnp.where(is_fwd, cs - log_dt, ec + log_dt) * LOG2_E).T
        r2_ref[0, :, sl] = (jnp.log(dt_pair) * LOG2_E).T
        r4_ref[0, :, sl] = etot.T


def _ssd_prep(dt_raw, bias, acoef, tl=1024):
    bsz, seq, w = dt_raw.shape
    col_in = pl.BlockSpec((1, tl, w), lambda b, i: (b, i, 0))
    col = pl.BlockSpec((1, tl, 2 * w), lambda b, i: (b, i, 0))
    rowm = pl.BlockSpec((1, w, tl), lambda b, i: (b, 0, i))
    col_shape = jax.ShapeDtypeStruct((bsz, seq, 2 * w), BF16)
    row_shape = jax.ShapeDtypeStruct((bsz, w, seq), F32)
    return pl.pallas_call(
        functools.partial(_ssd_prep_kernel, tl=tl),
        grid=(bsz, seq // tl),
        in_specs=[col_in, pl.BlockSpec((1, w), lambda b, i: (0, 0)), pl.BlockSpec((1, w), lambda b, i: (0, 0))],
        out_specs=[col, col, col, rowm, rowm, rowm],
        out_shape=[col_shape, col_shape, col_shape, row_shape, row_shape, row_shape],
        compiler_params=_cparams(2),
        name="ssd_decay_tables",
    )(dt_raw, bias, acoef)


def _head_lanes(vals, lane):
    out = vals[SSD_HPG - 1]
    for k in range(SSD_HPG - 2, -1, -1):
        out = jnp.where(lane < (k + 1) * SSD_HEAD_DIM, vals[k], out)
    return out


def _ssd_core_kernel(xs_ref, bt_ref, cm_ref, z_ref, p1_ref, p2_ref, p3_ref, r1_ref, r2_ref, r4_ref,
                     dsk_ref, nw_ref, ef_ref, el_ref, o_ref, st_scr, xw_scr, *seg_scr, nchunks):
    q = SSD_CHUNK
    hpg = SSD_HPG
    lane = lax.broadcasted_iota(jnp.int32, (1, SSD_GW), 1)

    def chunk_decay(c, direction):
        blk = r4_ref[0, :, c * q:(c + 1) * q]
        rows = [jnp.concatenate([blk[direction * hpg + k:direction * hpg + k + 1, :]] * (SSD_GW // q), axis=1)
                for k in range(hpg)]
        return _head_lanes(rows, lane)

    def chunk_rows(c):
        return pl.ds(pl.multiple_of(c * q, q), q)

    xs_all = xs_ref[0].astype(F32)
    w3 = _dot(p3_ref[0], ef_ref[0])
    xw_scr[...] = (jnp.concatenate([xs_all, xs_all], axis=1) * w3).astype(BF16)

    nseg = len(seg_scr) // 2
    f2_seg, colv_seg = seg_scr[:nseg], seg_scr[nseg:]
    seg_chunks = nchunks // nseg

    def expand_output_tables(seg, j):
        rows = chunk_rows(seg * seg_chunks + j)
        local = chunk_rows(j)
        f2_seg[seg][local, :] = _dot(p2_ref[0, rows, :], ef_ref[0])
        colv_seg[seg][local, :] = _dot(p1_ref[0, rows, :], el_ref[0])

    for j in range(seg_chunks):
        expand_output_tables(0, j)

    def state_body(c, carry):
        rs = chunk_rows(c)
        st = _dot(bt_ref[:, rs], xw_scr[rs, :])
        st_scr[c, 0] = st[:, :SSD_GW]
        st_scr[c, 1] = st[:, SSD_GW:]
        return carry

    lax.fori_loop(0, nchunks, state_body, 0, unroll=True)

    s_f = jnp.zeros((SSD_STATE, SSD_GW), F32)
    for c in range(nchunks):
        add = st_scr[c, 0]
        st_scr[c, 0] = s_f
        s_f = chunk_decay(c, 0) * s_f + add
    s_b = jnp.zeros((SSD_STATE, SSD_GW), F32)
    for c in range(nchunks - 1, -1, -1):
        add = st_scr[c, 1]
        st_scr[c, 1] = s_b
        s_b = chunk_decay(c, 1) * s_b + add

    ti = lax.broadcasted_iota(jnp.int32, (q, q), 0)
    si = lax.broadcasted_iota(jnp.int32, (q, q), 1)
    lower_strict = si < ti
    upper_strict = si > ti

    def out_chunk(seg, j):
        c = seg * seg_chunks + j
        rs = chunk_rows(c)
        local = chunk_rows(j)
        colv_scr, f2_scr = colv_seg[seg], f2_seg[seg]
        xs_b = xs_ref[0, rs, :]
        cm = cm_ref[0, rs, :]
        g = _dot(cm, bt_ref[:, rs])
        r1 = r1_ref[0, :, rs]
        r2 = r2_ref[0, :, rs]
        ms = []
        for k in range(hpg):
            kb = hpg + k
            seg_exp = jnp.where(lower_strict, colv_scr[local, k * q:(k + 1) * q] - r1[k:k + 1, :],
                                jnp.where(upper_strict, r1[kb:kb + 1, :] - colv_scr[local, kb * q:(kb + 1) * q],
                                          r2[k:k + 1, :]))
            ms.append((g * jnp.exp2(seg_exp)).astype(BF16))
        zero = jnp.zeros_like(xs_b)
        xbd = jnp.concatenate(
            [jnp.where(jnp.logical_and(lane >= k * SSD_HEAD_DIM, lane < (k + 1) * SSD_HEAD_DIM), xs_b, zero)
             for k in range(hpg)], axis=0)
        y = _dot(jnp.concatenate(ms, axis=1), xbd)
        s_in = jnp.concatenate([st_scr[c, 0], st_scr[c, 1]], axis=1).astype(BF16)
        y_off = _dot(cm, s_in)
        xs = xs_b.astype(F32)
        f2 = f2_scr[local, :]
        y = y + f2[:, :SSD_GW] * y_off[:, :SSD_GW] + f2[:, SSD_GW:] * y_off[:, SSD_GW:] + dsk_ref[0] * xs
        z = z_ref[0, rs, :].astype(F32)
        y = y * _silu(z)
        y = y * lax.rsqrt(jnp.mean(y * y, axis=-1, keepdims=True) + EPS)
        o_ref[0, rs, :] = (y * nw_ref[0]).astype(BF16)

    for seg in range(nseg):
        def seg_body(j, carry, seg=seg):
            out_chunk(seg, j)
            if seg + 1 < nseg:
                expand_output_tables(seg + 1, j)
            return carry

        lax.fori_loop(0, seg_chunks, seg_body, 0, unroll=True)


def _expansion_matrices():
    nhd = 2 * SSD_HPG
    nlive = 2 * SSD_HEADS
    row = jnp.arange(2 * SSD_DT_PAD)[None, :, None]
    grp = jnp.arange(SSD_GROUPS)[:, None, None]
    col_of_row = row % nlive - grp * nhd
    live = row < 3 * nlive
    head_lane = jnp.arange(2 * SSD_GW)[None, None, :] // SSD_HEAD_DIM
    block_lane = jnp.arange(nhd * SSD_CHUNK)[None, None, :] // SSD_CHUNK
    ef = jnp.where(live & (col_of_row == head_lane), 1.0, 0.0).astype(BF16)
    el = jnp.where(live & (col_of_row == block_lane), 1.0, 0.0).astype(BF16)
    return ef, el


def _ssd_core(xs, bt, cm, z, p1, p2, p3, r1, r2, r4, dsk, nw):
    bsz, seq, _ = xs.shape
    nchunks = seq // SSD_CHUNK
    ef, el = _expansion_matrices()
    wide = pl.BlockSpec((1, seq, SSD_GW), lambda b, g: (b, 0, g))
    narrow = pl.BlockSpec((1, seq, SSD_STATE), lambda b, g: (b, 0, g))
    state_major = pl.BlockSpec((SSD_STATE, seq), lambda b, g: (g, b))
    colt = pl.BlockSpec((1, seq, 2 * SSD_DT_PAD), lambda b, g: (b, 0, 0))
    rowt = pl.BlockSpec((1, 2 * SSD_HPG, seq), lambda b, g: (b, g, 0))
    pergroup = pl.BlockSpec((1, 1, SSD_GW), lambda b, g: (g, 0, 0))
    return pl.pallas_call(
        functools.partial(_ssd_core_kernel, nchunks=nchunks),
        grid=(bsz, SSD_GROUPS),
        in_specs=[wide, state_major, narrow, wide, colt, colt, colt, rowt, rowt, rowt, pergroup, pergroup,
                  pl.BlockSpec((1,) + ef.shape[1:], lambda b, g: (g, 0, 0)),
                  pl.BlockSpec((1,) + el.shape[1:], lambda b, g: (g, 0, 0))],
        out_specs=wide,
        out_shape=jax.ShapeDtypeStruct((bsz, seq, SSD_D_INNER), BF16),
        scratch_shapes=([pltpu.VMEM((nchunks, 2, SSD_STATE, SSD_GW), F32),
                         pltpu.VMEM((seq, 2 * SSD_GW), BF16)]
                        + [pltpu.VMEM((seq // SSD_SCAN_SEGMENTS, 2 * SSD_GW), F32)] * SSD_SCAN_SEGMENTS
                        + [pltpu.VMEM((seq // SSD_SCAN_SEGMENTS, 2 * SSD_HPG * SSD_CHUNK), F32)] * SSD_SCAN_SEGMENTS),
        compiler_params=_cparams(2),
        name="ssd_scan",
    )(xs, bt, cm, z, p1, p2, p3, r1, r2, r4, dsk, nw, ef, el)


def _ssd_layer(x2d, bsz, seq, w_in, conv_w, conv_b, dt_bias, a_log, d_skip, norm_w, w_out, g, b, ffn, alpha):
    n = x2d.shape[0]
    o_xbc = SSD_D_INNER
    o_dt = SSD_D_INNER + SSD_CONV_DIM
    perm = jnp.array([dr * SSD_HEADS + gi * SSD_HPG + k
                      for gi in range(SSD_GROUPS) for dr in range(2) for k in range(SSD_HPG)], jnp.int32)
    pad = SSD_DT_PAD - 2 * SSD_HEADS
    wz = w_in[:, :o_xbc].astype(BF16)
    wxbc = w_in[:, o_xbc:o_dt].astype(BF16)
    wdt = jnp.pad(w_in[:, o_dt:][:, perm], ((0, 0), (0, pad))).astype(BF16)
    bias = jnp.pad(dt_bias.reshape(-1)[perm], (0, pad)).reshape(1, SSD_DT_PAD)
    acoef = jnp.pad(a_log.reshape(-1)[perm], (0, pad)).reshape(1, SSD_DT_PAD)
    acoef = jnp.where(jnp.arange(SSD_DT_PAD) < 2 * SSD_HEADS, -jnp.exp(acoef), 0.0).reshape(1, SSD_DT_PAD)

    z, xs, bt, cm, dt_raw = _ssd_in(x2d, seq, wz, wxbc, wdt, conv_w, conv_b)
    tables = _ssd_prep(dt_raw.reshape(bsz, seq, SSD_DT_PAD), bias, acoef)
    dsk = jnp.repeat(d_skip, SSD_HEAD_DIM).reshape(SSD_GROUPS, 1, SSD_GW)
    nw = norm_w.reshape(SSD_GROUPS, 1, SSD_GW)
    y = _ssd_core(xs.reshape(bsz, seq, -1), bt, cm.reshape(bsz, seq, -1),
                  z.reshape(bsz, seq, -1), *tables, dsk, nw)
    return _proj_mlp(y.reshape(n, SSD_D_INNER), x2d, w_out.astype(BF16), g, b, ffn, alpha)


def _qkv_kernel(x_ref, w_ref, q_ref, k_ref, v_ref, *, col_chunk):
    d = D_MODEL
    xb = x_ref[...].astype(BF16)
    for j in range(d // col_chunk):
        lo = j * col_chunk
        q_ref[:, lo:lo + col_chunk] = (_dot(xb, w_ref[:, lo:lo + col_chunk]) * NA_Q_SCALE).astype(BF16)
        k_ref[:, lo:lo + col_chunk] = _dot(xb, w_ref[:, d + lo:d + lo + col_chunk]).astype(BF16)
        v_ref[:, lo:lo + col_chunk] = _dot(xb, w_ref[:, 2 * d + lo:2 * d + lo + col_chunk]).astype(BF16)


def _qkv(x2d, w, tm=512, col_chunk=512):
    n, d = x2d.shape
    spec = pl.BlockSpec((tm, d), lambda i: (i, 0))
    shape = jax.ShapeDtypeStruct((n, d), BF16)
    return pl.pallas_call(
        functools.partial(_qkv_kernel, col_chunk=col_chunk),
        grid=(n // tm,),
        in_specs=[spec, _resident(w.shape)],
        out_specs=[spec, spec, spec],
        out_shape=[shape, shape, shape],
        compiler_params=_cparams(1),
        name="na_qkv",
    )(x2d, w)


def _na_core_kernel(q_ref, k_ref, v_ref, *rest, rows):
    bias_refs = rest[:NA_ROWS_PER_STEP]
    o_ref, s_scr, p_scr = rest[NA_ROWS_PER_STEP:]
    pair = 2 * NA_HEAD_DIM
    n_pairs = NA_HEADS // 2
    lo_half = lax.broadcasted_iota(jnp.int32, (1, pair), 1) < NA_HEAD_DIM
    keys = []
    for rr in range(NA_ROWS_PER_STEP):
        r = pl.program_id(1) * NA_ROWS_PER_STEP + rr
        start = pl.multiple_of(jnp.clip(r - NA_WIN_ROWS // 2, 0, rows - NA_WIN_ROWS) * GRID_W, GRID_W)
        keys.append(pl.ds(start, NA_KEYS))
    units = [(rr, hp) for rr in range(NA_ROWS_PER_STEP) for hp in range(n_pairs)]
    for u, (rr, hp) in enumerate(units):
        sl = slice(hp * pair, (hp + 1) * pair)
        qp = q_ref[0, rr, :, sl]
        zero = jnp.zeros_like(qp)
        qm = jnp.concatenate([jnp.where(lo_half, qp, zero), jnp.where(lo_half, zero, qp)], axis=0)
        s = lax.dot_general(qm, k_ref[0, keys[rr], sl], (((1,), (1,)), ((), ())), preferred_element_type=F32)
        s_scr[u] = s + bias_refs[rr][0, 2 * hp:2 * hp + 2].reshape(2 * GRID_W, NA_KEYS)
    for u, (rr, hp) in enumerate(units):
        sl = slice(hp * pair, (hp + 1) * pair)
        denoms = []
        for rb in range(0, 2 * GRID_W, NA_SOFTMAX_ROWS):
            s = s_scr[u, rb:rb + NA_SOFTMAX_ROWS, :]
            p = jnp.exp2(s - jnp.max(s, axis=-1, keepdims=True))
            denoms.append(jnp.sum(p, axis=-1, keepdims=True))
            p_scr[u, rb:rb + NA_SOFTMAX_ROWS, :] = p.astype(BF16)
        o = _dot(p_scr[u], v_ref[0, keys[rr], sl]) / jnp.concatenate(denoms, axis=0)
        o_ref[0, rr, :, sl] = jnp.where(lo_half, o[:GRID_W], o[GRID_W:]).astype(BF16)


def _na_core(q, k, v, bias, bsz, seq):
    rows = seq // GRID_W
    d = D_MODEL
    half = NA_WIN_ROWS // 2

    def bias_spec(rr):
        def idx(b, i):
            r = i * NA_ROWS_PER_STEP + rr
            return (jnp.where(r < half, r, jnp.where(r <= rows - half, half, r - (rows - NA_WIN_ROWS))), 0, 0, 0)
        return pl.BlockSpec((1, NA_HEADS, GRID_W, NA_KEYS), idx)

    n_units = NA_ROWS_PER_STEP * NA_HEADS // 2
    qspec = pl.BlockSpec((1, NA_ROWS_PER_STEP, GRID_W, d), lambda b, i: (b, i, 0, 0))
    kvspec = pl.BlockSpec((1, seq, d), lambda b, i: (b, 0, 0))
    return pl.pallas_call(
        functools.partial(_na_core_kernel, rows=rows),
        grid=(bsz, rows // NA_ROWS_PER_STEP),
        in_specs=[qspec, kvspec, kvspec] + [bias_spec(rr) for rr in range(NA_ROWS_PER_STEP)],
        out_specs=qspec,
        out_shape=jax.ShapeDtypeStruct((bsz, rows, GRID_W, d), BF16),
        scratch_shapes=[pltpu.VMEM((n_units, 2 * GRID_W, NA_KEYS), F32),
                        pltpu.VMEM((n_units, 2 * GRID_W, NA_KEYS), BF16)],
        compiler_params=_cparams(2),
        name="na_attention",
    )(q.reshape(bsz, rows, GRID_W, d), k.reshape(bsz, seq, d), v.reshape(bsz, seq, d),
      *([bias] * NA_ROWS_PER_STEP))


def _na_bias_table(rpb):
    w = GRID_W
    n_dr = 2 * NA_WIN_ROWS - 1
    ext = jnp.concatenate([jnp.repeat(rpb[:, :, :1], w - NA_WIN_COLS, axis=2), rpb.astype(F32),
                           jnp.repeat(rpb[:, :, -1:], 2 * w - (w - NA_WIN_COLS) - (2 * NA_WIN_COLS - 1), axis=2)],
                          axis=2)
    flat = jnp.broadcast_to(ext[:, :, None, :], (NA_HEADS, n_dr, w, 2 * w)).reshape(NA_HEADS, n_dr, 2 * w * w)
    toep = flat[:, :, w - 1:w - 1 + w * (2 * w - 1)].reshape(NA_HEADS, n_dr, w, 2 * w - 1)[..., :w]
    qc = jnp.arange(w)
    kc = jnp.arange(w)
    col_start = jnp.clip(qc - NA_WIN_COLS // 2, 0, w - NA_WIN_COLS)
    in_win = (kc[None, :] >= col_start[:, None]) & (kc[None, :] < col_start[:, None] + NA_WIN_COLS)
    toep = jnp.where(in_win[None, None], toep, -jnp.inf).transpose(0, 2, 1, 3)
    per_offset = [toep[:, :, NA_WIN_ROWS - 1 - o:2 * NA_WIN_ROWS - 1 - o]
                  for o in range(NA_WIN_ROWS)]
    return jnp.stack(per_offset, axis=0).reshape(NA_WIN_ROWS, NA_HEADS, w, NA_KEYS) * LOG2_E


def _na_layer(x2d, bsz, seq, w_qkv, rpb, w_out, g, b, ffn, alpha):
    n, d = x2d.shape
    q, k, v = _qkv(x2d, w_qkv.astype(BF16))
    o = _na_core(q, k, v, _na_bias_table(rpb), bsz, seq)
    return _proj_mlp(o.reshape(n, d), x2d, w_out.astype(BF16), g, b, ffn, alpha)


def kernel(x, ln_mix_g, ln_mix_b, ln_ffn_g, ln_ffn_b, ffn_w_in, ffn_w_out, ssd_w_in, ssd_conv_w, ssd_conv_b,
           ssd_dt_bias, ssd_a_log, ssd_d, ssd_norm_w, ssd_w_out, sc_w_in, sc_conv_w, sc_w_out,
           na_w_qkv, na_rpb, na_w_out):
    bsz, seq, d = x.shape
    alpha = (2.0 * DEPTH) ** 0.25
    h = x.reshape(bsz * seq, d)
    for i in range(DEPTH):
        kind = i % N_MIXERS
        j = i // N_MIXERS
        ffn = (ffn_w_in[i].astype(BF16), ffn_w_out[i].astype(BF16), ln_ffn_g[i], ln_ffn_b[i])
        if kind == 0:
            h = _ssd_layer(h, bsz, seq, ssd_w_in[j], ssd_conv_w[j], ssd_conv_b[j], ssd_dt_bias[j], ssd_a_log[j],
                           ssd_d[j], ssd_norm_w[j], ssd_w_out[j], ln_mix_g[i], ln_mix_b[i], ffn, alpha)
        elif kind == 1:
            h = _sc_layer(h, seq, sc_w_in[j].astype(BF16), sc_conv_w[j], sc_w_out[j].astype(BF16),
                          ln_mix_g[i], ln_mix_b[i], ffn, alpha)
        else:
            h = _na_layer(h, bsz, seq, na_w_qkv[j], na_rpb[j], na_w_out[j], ln_mix_g[i], ln_mix_b[i], ffn, alpha)
    return h.reshape(bsz, seq, d)
```

```python
import functools

import jax
import jax.numpy as jnp
from jax import lax
from jax.experimental import pallas as pl
from jax.experimental.pallas import tpu as pltpu

F32 = jnp.float32
BF16 = jnp.bfloat16

D_MODEL = 1024
DEPTH = 4
N_MIXERS = 3
FFN_DIM = 4 * D_MODEL
EPS = 1e-5

SSD_D_INNER = 2 * D_MODEL
SSD_HEAD_DIM = 64
SSD_HEADS = SSD_D_INNER // SSD_HEAD_DIM
SSD_GROUPS = 8
SSD_HPG = SSD_HEADS // SSD_GROUPS
SSD_GW = SSD_HPG * SSD_HEAD_DIM
SSD_STATE = 128
SSD_CONV = 5
SSD_CHUNK = 128
SSD_GN = SSD_GROUPS * SSD_STATE
SSD_CONV_DIM = SSD_D_INNER + 2 * SSD_GN
SSD_DT_PAD = 128

SC_CONV = 3

GRID_W = 64
NA_HEADS = 16
NA_HEAD_DIM = D_MODEL // NA_HEADS
NA_WIN_ROWS = 8
NA_WIN_COLS = 16
NA_KEYS = NA_WIN_ROWS * GRID_W
NA_ROWS_PER_STEP = 2
NA_SOFTMAX_ROWS = 32
LOG2_E = 1.4426950408889634
NA_Q_SCALE = NA_HEAD_DIM ** -0.5 * LOG2_E

HALO = 8
LANES = 128
SSD_IN_YBUFS = 2
VMEM_LIMIT = 56 * 1024 * 1024


def _cparams(n_axes):
    return pltpu.CompilerParams(dimension_semantics=("arbitrary",) * n_axes,
                                vmem_limit_bytes=VMEM_LIMIT)


def _resident(shape):
    nd = len(shape)
    return pl.BlockSpec(shape, lambda *_: (0,) * nd, pipeline_mode=pl.Buffered(1))


def _dot(a, b):
    return jnp.dot(a, b, preferred_element_type=F32)


def _silu(x):
    h = 0.5 * x
    return h + h * jnp.tanh(h)


def _layer_norm(y, g, b):
    mu = jnp.mean(y, axis=-1, keepdims=True)
    yc = y - mu
    var = jnp.mean(yc * yc, axis=-1, keepdims=True)
    return yc * lax.rsqrt(var + EPS) * g + b


def _halo_specs(tm, n_rows, width):
    blocks_per_tile = tm // HALO
    last_block = n_rows // HALO - 1
    prev = pl.BlockSpec((HALO, width), lambda i: (jnp.maximum(i * blocks_per_tile - 1, 0), 0))
    cur = pl.BlockSpec((tm, width), lambda i: (i, 0))
    nxt = pl.BlockSpec((HALO, width), lambda i: (jnp.minimum((i + 1) * blocks_per_tile, last_block), 0))
    return prev, cur, nxt


def _halo_valid(tm, tiles_per_seq):
    i = pl.program_id(0)
    first = (i % tiles_per_seq) == 0
    last = (i % tiles_per_seq) == tiles_per_seq - 1
    rows = lax.broadcasted_iota(jnp.int32, (tm + 2 * HALO, 1), 0)
    ok_top = jnp.logical_or(rows >= HALO, jnp.logical_not(first))
    ok_bot = jnp.logical_or(rows < tm + HALO, jnp.logical_not(last))
    return jnp.logical_and(ok_top, ok_bot)


FF_CHUNK = 1024


def _mlp_ln_tail(x1, w1_ref, w2_ref, g_ref, b_ref, alpha):
    xb = x1.astype(BF16)
    acc = alpha * x1
    for c in range(FFN_DIM // FF_CHUNK):
        sl = slice(c * FF_CHUNK, (c + 1) * FF_CHUNK)
        h = jnp.maximum(_dot(xb, w1_ref[:, sl]), 0.0)
        acc = acc + _dot((h * h).astype(BF16), w2_ref[sl, :])
    return _layer_norm(acc, g_ref[...], b_ref[...])


def _ffn_operands(ffn, d):
    w1, w2, g, b = ffn
    return ([w1, w2, g.reshape(1, d), b.reshape(1, d)],
            [_resident(w1.shape), _resident(w2.shape), _resident((1, d)), _resident((1, d))])


def _proj_mlp_kernel(y_ref, x_ref, w_ref, g_ref, b_ref, w1_ref, w2_ref, g2_ref, b2_ref, o_ref, *, alpha):
    x1 = _layer_norm(alpha * x_ref[...] + _dot(y_ref[...], w_ref[...]), g_ref[...], b_ref[...])
    o_ref[...] = _mlp_ln_tail(x1, w1_ref, w2_ref, g2_ref, b2_ref, alpha)


def _proj_mlp(y2d, x2d, w, g, b, ffn, alpha, tm=512):
    n, d = x2d.shape
    k = y2d.shape[1]
    ffn_args, ffn_specs = _ffn_operands(ffn, d)
    return pl.pallas_call(
        functools.partial(_proj_mlp_kernel, alpha=alpha),
        grid=(n // tm,),
        in_specs=[pl.BlockSpec((tm, k), lambda i: (i, 0)),
                  pl.BlockSpec((tm, d), lambda i: (i, 0)),
                  _resident(w.shape), _resident((1, d)), _resident((1, d))] + ffn_specs,
        out_specs=pl.BlockSpec((tm, d), lambda i: (i, 0)),
        out_shape=jax.ShapeDtypeStruct((n, d), F32),
        compiler_params=_cparams(1),
        name="proj_ln_mlp_ln",
    )(y2d, x2d, w, g.reshape(1, d), b.reshape(1, d), *ffn_args)


def _sc_kernel(xp_ref, x_ref, xn_ref, win_ref, cw_ref, wout_ref, g_ref, b_ref, w1_ref, w2_ref, g2_ref, b2_ref,
               o_ref, v_scr, *, tm, tiles_per_seq, alpha, col_chunk):
    d = D_MODEL
    n = tm + 2 * HALO
    x = x_ref[...]
    xcat = jnp.concatenate([xp_ref[...], x, xn_ref[...]], axis=0).astype(BF16)
    xb = xcat[HALO:HALO + tm]
    valid = _halo_valid(tm, tiles_per_seq)
    for j in range(d // col_chunk):
        lo = j * col_chunk
        b_gate = _dot(xb, win_ref[:, lo:lo + col_chunk])
        c_gate = _dot(xcat, win_ref[:, d + lo:d + lo + col_chunk])
        h = _dot(xcat, win_ref[:, 2 * d + lo:2 * d + lo + col_chunk])
        u = jnp.where(valid, c_gate * h, 0.0)
        cw = cw_ref[:, lo:lo + col_chunk]
        conv = (cw[0:1] * pltpu.roll(u, 1, 0) + cw[1:2] * u + cw[2:3] * pltpu.roll(u, n - 1, 0))
        v_scr[:, lo:lo + col_chunk] = (b_gate * conv[HALO:HALO + tm]).astype(BF16)
    x1 = _layer_norm(alpha * x + _dot(v_scr[...], wout_ref[...]), g_ref[...], b_ref[...])
    o_ref[...] = _mlp_ln_tail(x1, w1_ref, w2_ref, g2_ref, b2_ref, alpha)


def _sc_layer(x2d, seq, w_in, conv_w, w_out, g, b, ffn, alpha, tm=512, col_chunk=256):
    n, d = x2d.shape
    prev, cur, nxt = _halo_specs(tm, n, d)
    ffn_args, ffn_specs = _ffn_operands(ffn, d)
    return pl.pallas_call(
        functools.partial(_sc_kernel, tm=tm, tiles_per_seq=seq // tm, alpha=alpha, col_chunk=col_chunk),
        grid=(n // tm,),
        in_specs=[prev, cur, nxt, _resident(w_in.shape), _resident(conv_w.shape), _resident(w_out.shape),
                  _resident((1, d)), _resident((1, d))] + ffn_specs,
        out_specs=pl.BlockSpec((tm, d), lambda i: (i, 0)),
        out_shape=jax.ShapeDtypeStruct((n, d), F32),
        scratch_shapes=[pltpu.VMEM((tm, d), BF16)],
        compiler_params=_cparams(1),
        name="short_conv_mlp_layer",
    )(x2d, x2d, x2d, w_in, conv_w, w_out, g.reshape(1, d), b.reshape(1, d), *ffn_args)


def _ssd_in_kernel(xp_ref, x_ref, xn_ref, wz_ref, wxbc_ref, wdt_ref, cw_ref, cb_ref,
                   z_ref, xs_ref, bt_ref, cm_ref, dt_ref, y_scr, *, tm, tiles_per_seq, col_chunk):
    n = tm + 2 * HALO
    xcat = jnp.concatenate([xp_ref[...], x_ref[...], xn_ref[...]], axis=0).astype(BF16)
    xb = xcat[HALO:HALO + tm]
    valid = _halo_valid(tm, tiles_per_seq)
    n_conv = SSD_CONV_DIM // col_chunk
    n_gate = SSD_D_INNER // col_chunk
    for j in range(n_conv):
        if j % (n_conv // n_gate) == 0:
            glo = (j // (n_conv // n_gate)) * col_chunk
            z_ref[:, glo:glo + col_chunk] = _dot(xb, wz_ref[:, glo:glo + col_chunk]).astype(BF16)
        if j == n_conv - 1:
            dt_ref[...] = _dot(xb, wdt_ref[...])
        lo = j * col_chunk
        y = _dot(xcat, wxbc_ref[:, lo:lo + col_chunk])
        y = jnp.concatenate([jnp.where(valid[:HALO], y[:HALO], 0.0), y[HALO:HALO + tm],
                             jnp.where(valid[HALO + tm:], y[HALO + tm:], 0.0)], axis=0)
        half = SSD_CONV // 2
        convs = []
        for s in range(col_chunk // LANES):
            yb = y_scr.at[j % SSD_IN_YBUFS, s]
            yb[...] = y[:, s * LANES:(s + 1) * LANES]
            cw = cw_ref[:, lo + s * LANES:lo + (s + 1) * LANES]
            conv = cb_ref[:, lo + s * LANES:lo + (s + 1) * LANES]
            for k in range(SSD_CONV):
                conv = conv + cw[k:k + 1] * yb[HALO + k - half:HALO + k - half + tm, :]
            convs.append(conv)
        act = _silu(jnp.concatenate(convs, axis=1))
        if lo < SSD_D_INNER:
            xs_ref[:, lo:lo + col_chunk] = act.astype(BF16)
        elif lo < SSD_D_INNER + SSD_GN:
            o = lo - SSD_D_INNER
            bt_ref[o:o + col_chunk, :] = act.T.astype(BF16)
        else:
            o = lo - SSD_D_INNER - SSD_GN
            cm_ref[:, o:o + col_chunk] = act.astype(BF16)


def _ssd_in(x2d, seq, wz, wxbc, wdt, conv_w, conv_b, tm=1024, col_chunk=256):
    n, d = x2d.shape
    prev, cur, nxt = _halo_specs(tm, n, d)
    row = lambda w: pl.BlockSpec((tm, w), lambda i: (i, 0))
    return pl.pallas_call(
        functools.partial(_ssd_in_kernel, tm=tm, tiles_per_seq=seq // tm, col_chunk=col_chunk),
        grid=(n // tm,),
        in_specs=[prev, cur, nxt, _resident(wz.shape), _resident(wxbc.shape), _resident(wdt.shape),
                  _resident(conv_w.shape), _resident((1, SSD_CONV_DIM))],
        out_specs=[row(SSD_D_INNER), row(SSD_D_INNER), pl.BlockSpec((SSD_GN, tm), lambda i: (0, i)),
                   row(SSD_GN), row(SSD_DT_PAD)],
        out_shape=[jax.ShapeDtypeStruct((n, SSD_D_INNER), BF16), jax.ShapeDtypeStruct((n, SSD_D_INNER), BF16),
                   jax.ShapeDtypeStruct((SSD_GN, n), BF16), jax.ShapeDtypeStruct((n, SSD_GN), BF16),
                   jax.ShapeDtypeStruct((n, SSD_DT_PAD), F32)],
        scratch_shapes=[pltpu.VMEM((SSD_IN_YBUFS, col_chunk // LANES, tm + 2 * HALO, LANES), F32)],
        compiler_params=_cparams(1),
        name="ssd_in_proj_conv",
    )(x2d, x2d, x2d, wz, wxbc, wdt, conv_w, conv_b.reshape(1, SSD_CONV_DIM))


def _split3(v):
    nlive = 2 * SSD_HEADS
    live = lax.broadcasted_iota(jnp.int32, (1, SSD_DT_PAD), 1) < nlive
    hi = v.astype(BF16)
    rem = v - hi.astype(F32)
    rem2 = rem - rem.astype(BF16).astype(F32)
    first = jnp.where(live, v, pltpu.roll(rem, nlive, 1)).astype(BF16)
    second = jnp.where(live, rem2, 0.0).astype(BF16)
    return jnp.concatenate([first, second], axis=1)


def _ssd_prep_kernel(dt_ref, bias_ref, acoef_ref, p1_ref, p2_ref, p3_ref, r1_ref, r2_ref, r4_ref, *, tl):
    rows = lax.broadcasted_iota(jnp.int32, (SSD_CHUNK, 1), 0)
    lane = lax.broadcasted_iota(jnp.int32, (1, SSD_DT_PAD), 1)
    is_fwd = (lane // SSD_HPG) % 2 == 0
    for c in range(tl // SSD_CHUNK):
        sl = slice(c * SSD_CHUNK, (c + 1) * SSD_CHUNK)
        raw = dt_ref[0, sl, :] + bias_ref[...]
        dt = jnp.maximum(raw, 0.0) + jnp.log1p(jnp.exp(-jnp.abs(raw)))
        a = dt * acoef_ref[...]
        cs = a
        sh = 1
        while sh < SSD_CHUNK:
            cs = cs + jnp.where(rows >= sh, pltpu.roll(cs, sh, 0), 0.0)
            sh *= 2
        tot = cs[SSD_CHUNK - 1:SSD_CHUNK, :]
        ec = cs - a
        p1 = jnp.where(is_fwd, cs, ec)
        p2 = jnp.where(is_fwd, jnp.exp(cs), jnp.exp(tot - ec))
        p3 = jnp.where(is_fwd, jnp.exp(tot - cs), jnp.exp(ec)) * dt
        etot = jnp.broadcast_to(jnp.exp(tot), (SSD_CHUNK, SSD_DT_PAD))
        p1_ref[0, sl, :] = _split3(p1 * LOG2_E)
        p2_ref[0, sl, :] = _split3(p2)
        p3_ref[0, sl, :] = _split3(p3)
        log_dt = jnp.log(dt)
        dt_pair = dt + jnp.where(is_fwd, pltpu.roll(dt, SSD_DT_PAD - SSD_HPG, 1), pltpu.roll(dt, SSD_HPG, 1))
        r1_ref[0, :, sl] = (jnp.where(is_fwd, cs - log_dt, ec + log_dt) * LOG2_E).T
        r2_ref[0, :, sl] = (jnp.log(dt_pair) * LOG2_E).T
        r4_ref[0, :, sl] = etot.T


def _ssd_prep(dt_raw, bias, acoef, tl=1024):
    bsz, seq, w = dt_raw.shape
    col_in = pl.BlockSpec((1, tl, w), lambda b, i: (b, i, 0))
    col = pl.BlockSpec((1, tl, 2 * w), lambda b, i: (b, i, 0))
    rowm = pl.BlockSpec((1, w, tl), lambda b, i: (b, 0, i))
    col_shape = jax.ShapeDtypeStruct((bsz, seq, 2 * w), BF16)
    row_shape = jax.ShapeDtypeStruct((bsz, w, seq), F32)
    return pl.pallas_call(
        functools.partial(_ssd_prep_kernel, tl=tl),
        grid=(bsz, seq // tl),
        in_specs=[col_in, pl.BlockSpec((1, w), lambda b, i: (0, 0)), pl.BlockSpec((1, w), lambda b, i: (0, 0))],
        out_specs=[col, col, col, rowm, rowm, rowm],
        out_shape=[col_shape, col_shape, col_shape, row_shape, row_shape, row_shape],
        compiler_params=_cparams(2),
        name="ssd_decay_tables",
    )(dt_raw, bias, acoef)


def _head_lanes(vals, lane):
    out = vals[SSD_HPG - 1]
    for k in range(SSD_HPG - 2, -1, -1):
        out = jnp.where(lane < (k + 1) * SSD_HEAD_DIM, vals[k], out)
    return out


def _ssd_core_kernel(xs_ref, bt_ref, cm_ref, z_ref, p1_ref, p2_ref, p3_ref, r1_ref, r2_ref, r4_ref,
                     dsk_ref, nw_ref, ef_ref, el_ref, o_ref, st_scr, xw_scr, f2_scr, colv_scr, *, nchunks):
    q = SSD_CHUNK
    hpg = SSD_HPG
    lane = lax.broadcasted_iota(jnp.int32, (1, SSD_GW), 1)
    chunks = [slice(c * q, (c + 1) * q) for c in range(nchunks)]

    def chunk_decay(c, direction):
        blk = r4_ref[0, :, chunks[c]]
        rows = [jnp.concatenate([blk[direction * hpg + k:direction * hpg + k + 1, :]] * (SSD_GW // q), axis=1)
                for k in range(hpg)]
        return _head_lanes(rows, lane)

    xs_all = xs_ref[0].astype(F32)
    w3 = _dot(p3_ref[0], ef_ref[0])
    xw_scr[...] = (jnp.concatenate([xs_all, xs_all], axis=1) * w3).astype(BF16)
    for rs in chunks:
        f2_scr[rs, :] = _dot(p2_ref[0, rs, :], ef_ref[0])
        colv_scr[rs, :] = _dot(p1_ref[0, rs, :], el_ref[0])

    for c, rs in enumerate(chunks):
        st = _dot(bt_ref[:, rs], xw_scr[rs, :])
        st_scr[c, 0] = st[:, :SSD_GW]
        st_scr[c, 1] = st[:, SSD_GW:]

    s_f = jnp.zeros((SSD_STATE, SSD_GW), F32)
    for c in range(nchunks):
        add = st_scr[c, 0]
        st_scr[c, 0] = s_f
        s_f = chunk_decay(c, 0) * s_f + add
    s_b = jnp.zeros((SSD_STATE, SSD_GW), F32)
    for c in range(nchunks - 1, -1, -1):
        add = st_scr[c, 1]
        st_scr[c, 1] = s_b
        s_b = chunk_decay(c, 1) * s_b + add

    ti = lax.broadcasted_iota(jnp.int32, (q, q), 0)
    si = lax.broadcasted_iota(jnp.int32, (q, q), 1)
    lower_strict = si < ti
    upper_strict = si > ti

    for c, rs in enumerate(chunks):
        xs_b = xs_ref[0, rs, :]
        cm = cm_ref[0, rs, :]
        g = _dot(cm, bt_ref[:, rs])
        r1 = r1_ref[0, :, rs]
        r2 = r2_ref[0, :, rs]
        ms = []
        for k in range(hpg):
            kb = hpg + k
            exponent = jnp.where(lower_strict, colv_scr[rs, k * q:(k + 1) * q] - r1[k:k + 1, :],
                                 jnp.where(upper_strict, r1[kb:kb + 1, :] - colv_scr[rs, kb * q:(kb + 1) * q],
                                           r2[k:k + 1, :]))
            ms.append((g * jnp.exp2(exponent)).astype(BF16))
        zero = jnp.zeros_like(xs_b)
        xbd = jnp.concatenate(
            [jnp.where(jnp.logical_and(lane >= k * SSD_HEAD_DIM, lane < (k + 1) * SSD_HEAD_DIM), xs_b, zero)
             for k in range(hpg)], axis=0)
        y = _dot(jnp.concatenate(ms, axis=1), xbd)
        s_in = jnp.concatenate([st_scr[c, 0], st_scr[c, 1]], axis=1).astype(BF16)
        y_off = _dot(cm, s_in)
        xs = xs_b.astype(F32)
        f2 = f2_scr[rs, :]
        y = y + f2[:, :SSD_GW] * y_off[:, :SSD_GW] + f2[:, SSD_GW:] * y_off[:, SSD_GW:] + dsk_ref[0] * xs
        z = z_ref[0, rs, :].astype(F32)
        y = y * _silu(z)
        y = y * lax.rsqrt(jnp.mean(y * y, axis=-1, keepdims=True) + EPS)
        o_ref[0, rs, :] = (y * nw_ref[0]).astype(BF16)


def _expansion_matrices():
    nhd = 2 * SSD_HPG
    nlive = 2 * SSD_HEADS
    row = jnp.arange(2 * SSD_DT_PAD)[None, :, None]
    grp = jnp.arange(SSD_GROUPS)[:, None, None]
    col_of_row = row % nlive - grp * nhd
    live = row < 3 * nlive
    head_lane = jnp.arange(2 * SSD_GW)[None, None, :] // SSD_HEAD_DIM
    block_lane = jnp.arange(nhd * SSD_CHUNK)[None, None, :] // SSD_CHUNK
    ef = jnp.where(live & (col_of_row == head_lane), 1.0, 0.0).astype(BF16)
    el = jnp.where(live & (col_of_row == block_lane), 1.0, 0.0).astype(BF16)
    return ef, el


def _ssd_core(xs, bt, cm, z, p1, p2, p3, r1, r2, r4, dsk, nw):
    bsz, seq, _ = xs.shape
    nchunks = seq // SSD_CHUNK
    ef, el = _expansion_matrices()
    wide = pl.BlockSpec((1, seq, SSD_GW), lambda b, g: (b, 0, g))
    narrow = pl.BlockSpec((1, seq, SSD_STATE), lambda b, g: (b, 0, g))
    state_major = pl.BlockSpec((SSD_STATE, seq), lambda b, g: (g, b))
    colt = pl.BlockSpec((1, seq, 2 * SSD_DT_PAD), lambda b, g: (b, 0, 0))
    rowt = pl.BlockSpec((1, 2 * SSD_HPG, seq), lambda b, g: (b, g, 0))
    pergroup = pl.BlockSpec((1, 1, SSD_GW), lambda b, g: (g, 0, 0))
    return pl.pallas_call(
        functools.partial(_ssd_core_kernel, nchunks=nchunks),
        grid=(bsz, SSD_GROUPS),
        in_specs=[wide, state_major, narrow, wide, colt, colt, colt, rowt, rowt, rowt, pergroup, pergroup,
                  pl.BlockSpec((1,) + ef.shape[1:], lambda b, g: (g, 0, 0)),
                  pl.BlockSpec((1,) + el.shape[1:], lambda b, g: (g, 0, 0))],
        out_specs=wide,
        out_shape=jax.ShapeDtypeStruct((bsz, seq, SSD_D_INNER), BF16),
        scratch_shapes=[pltpu.VMEM((nchunks, 2, SSD_STATE, SSD_GW), F32),
                        pltpu.VMEM((seq, 2 * SSD_GW), BF16),
                        pltpu.VMEM((seq, 2 * SSD_GW), F32),
                        pltpu.VMEM((seq, 2 * SSD_HPG * SSD_CHUNK), F32)],
        compiler_params=_cparams(2),
        name="ssd_scan",
    )(xs, bt, cm, z, p1, p2, p3, r1, r2, r4, dsk, nw, ef, el)


def _ssd_layer(x2d, bsz, seq, w_in, conv_w, conv_b, dt_bias, a_log, d_skip, norm_w, w_out, g, b, ffn, alpha):
    n = x2d.shape[0]
    o_xbc = SSD_D_INNER
    o_dt = SSD_D_INNER + SSD_CONV_DIM
    perm = jnp.array([dr * SSD_HEADS + gi * SSD_HPG + k
                      for gi in range(SSD_GROUPS) for dr in range(2) for k in range(SSD_HPG)], jnp.int32)
    pad = SSD_DT_PAD - 2 * SSD_HEADS
    wz = w_in[:, :o_xbc].astype(BF16)
    wxbc = w_in[:, o_xbc:o_dt].astype(BF16)
    wdt = jnp.pad(w_in[:, o_dt:][:, perm], ((0, 0), (0, pad))).astype(BF16)
    bias = jnp.pad(dt_bias.reshape(-1)[perm], (0, pad)).reshape(1, SSD_DT_PAD)
    acoef = jnp.pad(a_log.reshape(-1)[perm], (0, pad)).reshape(1, SSD_DT_PAD)
    acoef = jnp.where(jnp.arange(SSD_DT_PAD) < 2 * SSD_HEADS, -jnp.exp(acoef), 0.0).reshape(1, SSD_DT_PAD)

    z, xs, bt, cm, dt_raw = _ssd_in(x2d, seq, wz, wxbc, wdt, conv_w, conv_b)
    tables = _ssd_prep(dt_raw.reshape(bsz, seq, SSD_DT_PAD), bias, acoef)
    dsk = jnp.repeat(d_skip, SSD_HEAD_DIM).reshape(SSD_GROUPS, 1, SSD_GW)
    nw = norm_w.reshape(SSD_GROUPS, 1, SSD_GW)
    y = _ssd_core(xs.reshape(bsz, seq, -1), bt, cm.reshape(bsz, seq, -1),
                  z.reshape(bsz, seq, -1), *tables, dsk, nw)
    return _proj_mlp(y.reshape(n, SSD_D_INNER), x2d, w_out.astype(BF16), g, b, ffn, alpha)


def _qkv_kernel(x_ref, w_ref, q_ref, k_ref, v_ref, *, col_chunk):
    d = D_MODEL
    xb = x_ref[...].astype(BF16)
    for j in range(d // col_chunk):
        lo = j * col_chunk
        q_ref[:, lo:lo + col_chunk] = (_dot(xb, w_ref[:, lo:lo + col_chunk]) * NA_Q_SCALE).astype(BF16)
        k_ref[:, lo:lo + col_chunk] = _dot(xb, w_ref[:, d + lo:d + lo + col_chunk]).astype(BF16)
        v_ref[:, lo:lo + col_chunk] = _dot(xb, w_ref[:, 2 * d + lo:2 * d + lo + col_chunk]).astype(BF16)


def _qkv(x2d, w, tm=512, col_chunk=512):
    n, d = x2d.shape
    spec = pl.BlockSpec((tm, d), lambda i: (i, 0))
    shape = jax.ShapeDtypeStruct((n, d), BF16)
    return pl.pallas_call(
        functools.partial(_qkv_kernel, col_chunk=col_chunk),
        grid=(n // tm,),
        in_specs=[spec, _resident(w.shape)],
        out_specs=[spec, spec, spec],
        out_shape=[shape, shape, shape],
        compiler_params=_cparams(1),
        name="na_qkv",
    )(x2d, w)


def _na_core_kernel(q_ref, k_ref, v_ref, *rest, rows):
    bias_refs = rest[:NA_ROWS_PER_STEP]
    o_ref, s_scr, p_scr = rest[NA_ROWS_PER_STEP:]
    pair = 2 * NA_HEAD_DIM
    n_pairs = NA_HEADS // 2
    lo_half = lax.broadcasted_iota(jnp.int32, (1, pair), 1) < NA_HEAD_DIM
    keys = []
    for rr in range(NA_ROWS_PER_STEP):
        r = pl.program_id(1) * NA_ROWS_PER_STEP + rr
        start = pl.multiple_of(jnp.clip(r - NA_WIN_ROWS // 2, 0, rows - NA_WIN_ROWS) * GRID_W, GRID_W)
        keys.append(pl.ds(start, NA_KEYS))
    units = [(rr, hp) for rr in range(NA_ROWS_PER_STEP) for hp in range(n_pairs)]
    for u, (rr, hp) in enumerate(units):
        sl = slice(hp * pair, (hp + 1) * pair)
        qp = q_ref[0, rr, :, sl]
        zero = jnp.zeros_like(qp)
        qm = jnp.concatenate([jnp.where(lo_half, qp, zero), jnp.where(lo_half, zero, qp)], axis=0)
        s = lax.dot_general(qm, k_ref[0, keys[rr], sl], (((1,), (1,)), ((), ())), preferred_element_type=F32)
        s_scr[u] = s + bias_refs[rr][0, 2 * hp:2 * hp + 2].reshape(2 * GRID_W, NA_KEYS)
    for u, (rr, hp) in enumerate(units):
        sl = slice(hp * pair, (hp + 1) * pair)
        denoms = []
        for rb in range(0, 2 * GRID_W, NA_SOFTMAX_ROWS):
            s = s_scr[u, rb:rb + NA_SOFTMAX_ROWS, :]
            p = jnp.exp2(s - jnp.max(s, axis=-1, keepdims=True))
            denoms.append(jnp.sum(p, axis=-1, keepdims=True))
            p_scr[u, rb:rb + NA_SOFTMAX_ROWS, :] = p.astype(BF16)
        o = _dot(p_scr[u], v_ref[0, keys[rr], sl]) / jnp.concatenate(denoms, axis=0)
        o_ref[0, rr, :, sl] = jnp.where(lo_half, o[:GRID_W], o[GRID_W:]).astype(BF16)


def _na_core(q, k, v, bias, bsz, seq):
    rows = seq // GRID_W
    d = D_MODEL
    half = NA_WIN_ROWS // 2

    def bias_spec(rr):
        def idx(b, i):
            r = i * NA_ROWS_PER_STEP + rr
            return (jnp.where(r < half, r, jnp.where(r <= rows - half, half, r - (rows - NA_WIN_ROWS))), 0, 0, 0)
        return pl.BlockSpec((1, NA_HEADS, GRID_W, NA_KEYS), idx)

    n_units = NA_ROWS_PER_STEP * NA_HEADS // 2
    qspec = pl.BlockSpec((1, NA_ROWS_PER_STEP, GRID_W, d), lambda b, i: (b, i, 0, 0))
    kvspec = pl.BlockSpec((1, seq, d), lambda b, i: (b, 0, 0))
    return pl.pallas_call(
        functools.partial(_na_core_kernel, rows=rows),
        grid=(bsz, rows // NA_ROWS_PER_STEP),
        in_specs=[qspec, kvspec, kvspec] + [bias_spec(rr) for rr in range(NA_ROWS_PER_STEP)],
        out_specs=qspec,
        out_shape=jax.ShapeDtypeStruct((bsz, rows, GRID_W, d), BF16),
        scratch_shapes=[pltpu.VMEM((n_units, 2 * GRID_W, NA_KEYS), F32),
                        pltpu.VMEM((n_units, 2 * GRID_W, NA_KEYS), BF16)],
        compiler_params=_cparams(2),
        name="na_attention",
    )(q.reshape(bsz, rows, GRID_W, d), k.reshape(bsz, seq, d), v.reshape(bsz, seq, d),
      *([bias] * NA_ROWS_PER_STEP))


def _na_bias_table(rpb):
    w = GRID_W
    n_dr = 2 * NA_WIN_ROWS - 1
    ext = jnp.concatenate([jnp.repeat(rpb[:, :, :1], w - NA_WIN_COLS, axis=2), rpb.astype(F32),
                           jnp.repeat(rpb[:, :, -1:], 2 * w - (w - NA_WIN_COLS) - (2 * NA_WIN_COLS - 1), axis=2)],
                          axis=2)
    flat = jnp.broadcast_to(ext[:, :, None, :], (NA_HEADS, n_dr, w, 2 * w)).reshape(NA_HEADS, n_dr, 2 * w * w)
    toep = flat[:, :, w - 1:w - 1 + w * (2 * w - 1)].reshape(NA_HEADS, n_dr, w, 2 * w - 1)[..., :w]
    qc = jnp.arange(w)
    kc = jnp.arange(w)
    col_start = jnp.clip(qc - NA_WIN_COLS // 2, 0, w - NA_WIN_COLS)
    in_win = (kc[None, :] >= col_start[:, None]) & (kc[None, :] < col_start[:, None] + NA_WIN_COLS)
    toep = jnp.where(in_win[None, None], toep, -jnp.inf).transpose(0, 2, 1, 3)
    per_offset = [toep[:, :, NA_WIN_ROWS - 1 - o:2 * NA_WIN_ROWS - 1 - o]
                  for o in range(NA_WIN_ROWS)]
    return jnp.stack(per_offset, axis=0).reshape(NA_WIN_ROWS, NA_HEADS, w, NA_KEYS) * LOG2_E


def _na_layer(x2d, bsz, seq, w_qkv, rpb, w_out, g, b, ffn, alpha):
    n, d = x2d.shape
    q, k, v = _qkv(x2d, w_qkv.astype(BF16))
    o = _na_core(q, k, v, _na_bias_table(rpb), bsz, seq)
    return _proj_mlp(o.reshape(n, d), x2d, w_out.astype(BF16), g, b, ffn, alpha)


def kernel(x, ln_mix_g, ln_mix_b, ln_ffn_g, ln_ffn_b, ffn_w_in, ffn_w_out, ssd_w_in, ssd_conv_w, ssd_conv_b,
           ssd_dt_bias, ssd_a_log, ssd_d, ssd_norm_w, ssd_w_out, sc_w_in, sc_conv_w, sc_w_out,
           na_w_qkv, na_rpb, na_w_out):
    bsz, seq, d = x.shape
    alpha = (2.0 * DEPTH) ** 0.25
    h = x.reshape(bsz * seq, d)
    for i in range(DEPTH):
        kind = i % N_MIXERS
        j = i // N_MIXERS
        ffn = (ffn_w_in[i].astype(BF16), ffn_w_out[i].astype(BF16), ln_ffn_g[i], ln_ffn_b[i])
        if kind == 0:
            h = _ssd_layer(h, bsz, seq, ssd_w_in[j], ssd_conv_w[j], ssd_conv_b[j], ssd_dt_bias[j], ssd_a_log[j],
                           ssd_d[j], ssd_norm_w[j], ssd_w_out[j], ln_mix_g[i], ln_mix_b[i], ffn, alpha)
        elif kind == 1:
            h = _sc_layer(h, seq, sc_w_in[j].astype(BF16), sc_conv_w[j], sc_w_out[j].astype(BF16),
                          ln_mix_g[i], ln_mix_b[i], ffn, alpha)
        else:
            h = _na_layer(h, bsz, seq, na_w_qkv[j], na_rpb[j], na_w_out[j], ln_mix_g[i], ln_mix_b[i], ffn, alpha)
    return h.reshape(bsz, seq, d)
```
